```python
import math
import jax
import jax.numpy as jnp
from jax import lax
import numpy as np

D_MODEL = 1024
BATCH = 8
SEQ = 4096
DEPTH = 4

MEM_LEN = 256
HEAD_DIM = 64
N_BRANCH = 4
BRANCH_WIDTH = D_MODEL // N_BRANCH
FOX_HEADS = BRANCH_WIDTH // HEAD_DIM
DSA_HEADS = BRANCH_WIDTH // HEAD_DIM
IDX_HEADS = 8
IDX_DIM = 32
IDX_TOPK_MAX = 256
DIFF_HEADS = BRANCH_WIDTH // HEAD_DIM
DIFF_QK_DIM = HEAD_DIM // 2
DIFF_V_DIM = HEAD_DIM
MEM_HEADS = BRANCH_WIDTH // HEAD_DIM
FFN_HIDDEN = -(-8 * D_MODEL // (3 * 256)) * 256
ROPE_THETA = 500000.0
ROPE_FRACTION = 4
Q_BLOCK = 128
NORM_EPS = 1e-6

IN_SPLITS = (
    FOX_HEADS * HEAD_DIM, FOX_HEADS * HEAD_DIM, FOX_HEADS * HEAD_DIM, FOX_HEADS,
    DSA_HEADS * HEAD_DIM, HEAD_DIM, HEAD_DIM,
    IDX_HEADS * IDX_DIM, IDX_DIM, IDX_HEADS,
    DIFF_HEADS * 2 * DIFF_QK_DIM, DIFF_HEADS * 2 * DIFF_QK_DIM, DIFF_HEADS * DIFF_V_DIM,
    MEM_HEADS * HEAD_DIM,
)
IN_COLS = sum(IN_SPLITS)

kernel_name = "hybrid_fox_dsa_diff_memory_block"

F32 = jnp.float32


def rms_norm(x, g):
    xf = x.astype(F32)
    y = xf * lax.rsqrt(jnp.mean(xf * xf, axis=-1, keepdims=True) + NORM_EPS)
    return (y * g.astype(F32)).astype(x.dtype)


def rope_tables(positions, rot_dim):
    inv_freq = ROPE_THETA ** (-jnp.arange(0, rot_dim, 2, dtype=F32) / rot_dim)
    ang = positions.astype(F32)[..., None] * inv_freq
    return jnp.cos(ang)[:, :, None, :], jnp.sin(ang)[:, :, None, :]


def apply_partial_rope(x, cos, sin):
    half = cos.shape[-1]
    xf = x.astype(F32)
    x1, x2, rest = xf[..., :half], xf[..., half:2 * half], xf[..., 2 * half:]
    out = jnp.concatenate([x1 * cos - x2 * sin, x2 * cos + x1 * sin, rest], axis=-1)
    return out.astype(x.dtype)


def _to_blocks(a):
    b, s = a.shape[0], a.shape[1]
    return jnp.swapaxes(a.reshape(b, s // Q_BLOCK, Q_BLOCK, *a.shape[2:]), 0, 1)


def _from_blocks(a):
    nb, b, qb = a.shape[0], a.shape[1], a.shape[2]
    return jnp.swapaxes(a, 0, 1).reshape(b, nb * qb, *a.shape[3:])


def _query_positions(blk):
    return blk * Q_BLOCK + jnp.arange(Q_BLOCK)


def _causal_mask(blk, n_keys):
    return jnp.arange(n_keys)[None, :] <= _query_positions(blk)[:, None]


def forgetting_attention(q, k, v, f_logit):
    n_keys = k.shape[1]
    cum = jnp.cumsum(jax.nn.log_sigmoid(f_logit.astype(F32)), axis=1)
    cum_k = jnp.swapaxes(cum, 1, 2)
    scale = HEAD_DIM ** -0.5

    def block(args):
        qb, cb, blk = args
        s = jnp.einsum('bqhd,bkhd->bhqk', qb, k).astype(F32) * scale
        s = s + jnp.swapaxes(cb, 1, 2)[..., None] - cum_k[:, :, None, :]
        s = jnp.where(_causal_mask(blk, n_keys), s, -jnp.inf)
        p = jax.nn.softmax(s, axis=-1)
        return jnp.einsum('bhqk,bkhd->bqhd', p.astype(v.dtype), v)

    nb = q.shape[1] // Q_BLOCK
    out = lax.map(block, (_to_blocks(q), _to_blocks(cum), jnp.arange(nb)))
    return _from_blocks(out)


def dsa_attention(q, k, v, q_idx, k_idx, w_idx):
    n_keys = k.shape[1]
    top_k = min(IDX_TOPK_MAX, n_keys // 4)
    w_idx = w_idx.astype(F32) * IDX_HEADS ** -0.5
    scale = HEAD_DIM ** -0.5
    gather = jax.vmap(lambda kk, ii: kk[ii])

    def block(args):
        qb, qib, wb, blk = args
        rel = jax.nn.relu(jnp.einsum('bqhd,bkd->bqhk', qib, k_idx).astype(F32) * IDX_DIM ** -0.5)
        score = jnp.einsum('bqhk,bqh->bqk', rel, wb)
        score = jnp.where(_causal_mask(blk, n_keys)[None], score, -jnp.inf)
        _, sel = lax.top_k(score, top_k)
        k_sel = gather(k, sel)
        v_sel = gather(v, sel)
        valid = sel <= _query_positions(blk)[None, :, None]
        s = jnp.einsum('bqhd,bqkd->bhqk', qb, k_sel).astype(F32) * scale
        s = jnp.where(valid[:, None], s, -jnp.inf)
        p = jax.nn.softmax(s, axis=-1)
        return jnp.einsum('bhqk,bqkd->bqhd', p.astype(v.dtype), v_sel)

    nb = q.shape[1] // Q_BLOCK
    out = lax.map(block, (_to_blocks(q), _to_blocks(q_idx), _to_blocks(w_idx), jnp.arange(nb)))
    return _from_blocks(out)


def differential_attention(q, k, v, lam, subln_g, lambda_init):
    n_keys = k.shape[1]
    scale = DIFF_QK_DIM ** -0.5

    def block(args):
        qb, blk = args
        s = jnp.einsum('bqhmd,bkhmd->bmhqk', qb, k).astype(F32) * scale
        s = jnp.where(_causal_mask(blk, n_keys), s, -jnp.inf)
        p = jax.nn.softmax(s, axis=-1)
        a = p[:, 0] - lam * p[:, 1]
        return jnp.einsum('bhqk,bkhd->bqhd', a.astype(v.dtype), v)

    nb = q.shape[1] // Q_BLOCK
    out = _from_blocks(lax.map(block, (_to_blocks(q), jnp.arange(nb))))
    return rms_norm(out, subln_g) * (1.0 - lambda_init)


def memory_attention(q, mem, g, w_kv):
    b, m = mem.shape[0], mem.shape[1]
    kv = (rms_norm(mem, g) @ w_kv).reshape(b, m, 2, MEM_HEADS, HEAD_DIM)
    k, v = kv[:, :, 0], kv[:, :, 1]
    s = jnp.einsum('bshd,bmhd->bhsm', q, k).astype(F32) * HEAD_DIM ** -0.5
    p = jax.nn.softmax(s, axis=-1)
    return jnp.einsum('bhsm,bmhd->bshd', p.astype(v.dtype), v)


def setup_inputs(seed: int = 0) -> dict:
    key = jax.random.key(seed)
    ks = jax.random.split(key, 20)

    def nrm(k, shape, fan_in):
        return jax.random.normal(k, shape, F32) * fan_in ** -0.5

    def gain(k, shape):
        return 1.0 + 0.02 * jax.random.normal(k, shape, F32)

    x = jax.random.normal(ks[0], (BATCH, SEQ, D_MODEL), F32)
    mem = jax.random.normal(ks[1], (BATCH, MEM_LEN, D_MODEL), F32)
    positions = (jax.random.randint(ks[2], (BATCH, 1), 0, 1024, dtype=jnp.int32)
                 + jnp.arange(SEQ, dtype=jnp.int32)[None, :])
    return {
        "x": x,
        "mem": mem,
        "positions": positions,
        "norm_mix_g": gain(ks[3], (DEPTH, D_MODEL)),
        "w_in": nrm(ks[4], (DEPTH, D_MODEL, IN_COLS), D_MODEL),
        "b_forget": jax.random.uniform(ks[5], (DEPTH, FOX_HEADS), F32, minval=2.0, maxval=5.0),
        "diff_lambda": 0.1 * jax.random.normal(ks[6], (DEPTH, 4, DIFF_QK_DIM), F32),
        "diff_subln_g": gain(ks[7], (DEPTH, DIFF_V_DIM)),
        "norm_mem_g": gain(ks[8], (DEPTH, D_MODEL)),
        "w_mem_kv": nrm(ks[9], (DEPTH, D_MODEL, 2 * MEM_HEADS * HEAD_DIM), D_MODEL),
        "w_branch": nrm(ks[10], (DEPTH, N_BRANCH, BRANCH_WIDTH, D_MODEL), BRANCH_WIDTH),
        "w_gate": nrm(ks[11], (DEPTH, D_MODEL, N_BRANCH * D_MODEL), D_MODEL),
        "b_gate": 0.02 * jax.random.normal(ks[12], (DEPTH, N_BRANCH * D_MODEL), F32),
        "w_out": nrm(ks[13], (DEPTH, D_MODEL, D_MODEL), D_MODEL),
        "norm_ffn_g": gain(ks[14], (DEPTH, D_MODEL)),
        "w_ffn_in": nrm(ks[15], (DEPTH, D_MODEL, 2 * FFN_HIDDEN), D_MODEL),
        "w_ffn_out": nrm(ks[16], (DEPTH, FFN_HIDDEN, D_MODEL), FFN_HIDDEN),
        "final_norm_g": gain(ks[17], (D_MODEL,)),
    }


def reference(x, mem, positions, norm_mix_g, w_in, b_forget, diff_lambda, diff_subln_g,
              norm_mem_g, w_mem_kv, w_branch, w_gate, b_gate, w_out, norm_ffn_g,
              w_ffn_in, w_ffn_out, final_norm_g):
    b, s = x.shape[0], x.shape[1]
    cos16, sin16 = rope_tables(positions, HEAD_DIM // ROPE_FRACTION)
    cos8, sin8 = rope_tables(positions, DIFF_QK_DIM // ROPE_FRACTION)

    for l in range(DEPTH):
        lambda_init = 0.8 - 0.6 * math.exp(-0.3 * l)
        h = rms_norm(x, norm_mix_g[l])
        z = h @ w_in[l]
        parts = []
        off = 0
        for n in IN_SPLITS:
            parts.append(z[..., off:off + n])
            off += n
        (qa, ka, va, fa, qb, kb, vb, qi, ki, wi, qc, kc, vc, qm) = parts

        o_a = forgetting_attention(qa.reshape(b, s, FOX_HEADS, HEAD_DIM),
                                   ka.reshape(b, s, FOX_HEADS, HEAD_DIM),
                                   va.reshape(b, s, FOX_HEADS, HEAD_DIM),
                                   fa + b_forget[l])

        qb = apply_partial_rope(qb.reshape(b, s, DSA_HEADS, HEAD_DIM), cos16, sin16)
        kb = apply_partial_rope(kb[:, :, None, :], cos16, sin16)[:, :, 0]
        qi = apply_partial_rope(qi.reshape(b, s, IDX_HEADS, IDX_DIM), cos8, sin8)
        ki = apply_partial_rope(ki[:, :, None, :], cos8, sin8)[:, :, 0]
        o_b = dsa_attention(qb, kb, vb, qi, ki, wi)

        qc = apply_partial_rope(qc.reshape(b, s, DIFF_HEADS * 2, DIFF_QK_DIM), cos8, sin8)
        kc = apply_partial_rope(kc.reshape(b, s, DIFF_HEADS * 2, DIFF_QK_DIM), cos8, sin8)
        lam_vec = diff_lambda[l].astype(F32)
        lam = (jnp.exp(jnp.sum(lam_vec[0] * lam_vec[1]))
               - jnp.exp(jnp.sum(lam_vec[2] * lam_vec[3])) + lambda_init)
        o_c = differential_attention(qc.reshape(b, s, DIFF_HEADS, 2, DIFF_QK_DIM),
                                     kc.reshape(b, s, DIFF_HEADS, 2, DIFF_QK_DIM),
                                     vc.reshape(b, s, DIFF_HEADS, DIFF_V_DIM),
                                     lam, diff_subln_g[l], lambda_init)

        o_m = memory_attention(qm.reshape(b, s, MEM_HEADS, HEAD_DIM), mem, norm_mem_g[l], w_mem_kv[l])

        o = jnp.stack([o_a.reshape(b, s, BRANCH_WIDTH), o_b.reshape(b, s, BRANCH_WIDTH),
                       o_c.reshape(b, s, BRANCH_WIDTH), o_m.reshape(b, s, BRANCH_WIDTH)], axis=2)
        y = jnp.einsum('bsnc,ncd->bsnd', o, w_branch[l])
        g = jax.nn.sigmoid(h @ w_gate[l] + b_gate[l]).reshape(b, s, N_BRANCH, D_MODEL)
        x = x + jnp.sum(g * y, axis=2) @ w_out[l]

        h2 = rms_norm(x, norm_ffn_g[l])
        u = h2 @ w_ffn_in[l]
        x = x + (jax.nn.silu(u[..., :FFN_HIDDEN]) * u[..., FFN_HIDDEN:]) @ w_ffn_out[l]

    return rms_norm(x, final_norm_g)
```

```python
import functools
import math

import numpy as np
import jax
import jax.numpy as jnp
from jax import lax
from jax.experimental import pallas as pl
from jax.experimental.pallas import tpu as pltpu

F32 = jnp.float32
BF16 = jnp.bfloat16
I32 = jnp.int32

HEAD_DIM = 64
N_BRANCH = 4
N_HEADS = 4
IDX_HEADS = 8
IDX_DIM = 32
IDX_TOPK_MAX = 256
DIFF_QK_DIM = 32
MEM_HEADS = 4
ROPE_THETA = 500000.0
NORM_EPS = 1e-6
LANES = 128
NEG = -1e30
INT_MIN = -2 ** 31

VMEM_LIMIT = 56 * 1024 * 1024

_SEG = {}
_off = 0
for _name, _w in (("qa", 512), ("ka", 512), ("va", 256), ("fa", 128), ("qb", 256), ("kb", 128),
                  ("vb", 128), ("qi", 256), ("ki", 256), ("wi", 128), ("qc", 256), ("kc", 256),
                  ("vc", 256), ("qm", 256)):
    _SEG[_name] = (_off, _off + _w)
    _off += _w
PROJ_COLS = _off

_DN_T = (((1,), (1,)), ((), ()))


def _dot(a, b):
    return jnp.dot(a, b, preferred_element_type=F32)


def _dot_t(a, b):
    return lax.dot_general(a, b, _DN_T, preferred_element_type=F32)


def _rms(x, g):
    return x * lax.rsqrt(jnp.mean(x * x, axis=-1, keepdims=True) + NORM_EPS) * g


def _cparams(sem):
    return pltpu.CompilerParams(dimension_semantics=sem, vmem_limit_bytes=VMEM_LIMIT)


def _split3(x):
    hi = x.astype(BF16)
    r = x - hi.astype(F32)
    mid = r.astype(BF16)
    lo = (r - mid.astype(F32)).astype(BF16)
    return hi, mid, lo


def _rope(z, c, s, half, period):
    lane = lax.broadcasted_iota(I32, z.shape, 1)
    up = pltpu.roll(z, LANES - half, 1)
    dn = pltpu.roll(z, half, 1)
    partner = jnp.where((lane & (period - 1)) < half, up, dn)
    return z * c + partner * s


def _proj_kernel(x_ref, g_ref, w_ref, bf_ref, c16_ref, s16_ref, c8_ref, s8_ref, eq_ref, ek_ref,
                 oq_ref, ok_ref,
                 qa_o, ka_o, va_o, qb_o, kb_o, vb_o, qi_o, ki_o, wi_o, qc_o, kc_o, vc_o, qm_o,
                 carry_ref, *, tiles_per_batch):
    i = pl.program_id(0)
    tm = x_ref.shape[0]
    h = _rms(x_ref[...], g_ref[...]).astype(BF16)

    def mm(name):
        lo, hi = _SEG[name]
        return _dot(h, w_ref[:, lo:hi])

    f = mm("fa") + bf_ref[...]
    ls = jnp.minimum(f, 0.0) - jnp.log(1.0 + jnp.exp(-jnp.abs(f)))
    r_i = lax.broadcasted_iota(I32, (tm, tm), 0)
    c_i = lax.broadcasted_iota(I32, (tm, tm), 1)
    ltri = jnp.where(c_i <= r_i, 1.0, 0.0).astype(BF16)
    p_hi, p_mid, p_lo = _split3(ls)
    pre = _dot(ltri, p_hi) + _dot(ltri, p_mid) + _dot(ltri, p_lo)

    @pl.when(i % tiles_per_batch == 0)
    def _():
        carry_ref[...] = jnp.zeros_like(carry_ref)

    cum = pre + carry_ref[...]
    carry_ref[...] = cum[tm - 1:tm, :]
    c_hi, c_mid, c_lo = _split3(cum)

    qa = mm("qa") * (HEAD_DIM ** -0.5)
    qa = qa + _dot(c_hi, eq_ref[0]) + _dot(c_mid, eq_ref[1]) + _dot(c_lo, eq_ref[2]) + oq_ref[...]
    qa_o[...] = qa.astype(BF16)
    ka = mm("ka")
    ka = ka + _dot(c_hi, ek_ref[0]) + _dot(c_mid, ek_ref[1]) + _dot(c_lo, ek_ref[2]) + ok_ref[...]
    ka_o[...] = ka.astype(BF16)
    va_o[...] = mm("va").astype(BF16)

    c16 = c16_ref[...]
    s16 = s16_ref[...]
    c8 = c8_ref[...]
    s8 = s8_ref[...]

    def roped(name, c, s, half, period, scale):
        z = mm(name)
        parts = []
        for k in range(z.shape[1] // LANES):
            r = _rope(z[:, k * LANES:(k + 1) * LANES], c, s, half, period)
            if scale != 1.0:
                r = r * scale
            parts.append(r.astype(BF16))
        return parts

    for k, r in enumerate(roped("qb", c16, s16, 8, 64, HEAD_DIM ** -0.5)):
        qb_o[:, k * LANES:(k + 1) * LANES] = r
    kb_o[...] = roped("kb", c16, s16, 8, 64, 1.0)[0]
    vb_o[...] = mm("vb").astype(BF16)
    for k, r in enumerate(roped("qi", c8, s8, 4, 32, 1.0)):
        qi_o[:, k * LANES:(k + 1) * LANES] = r
    for k, r in enumerate(roped("ki", c8, s8, 4, 32, 1.0)):
        ki_o[:, k * LANES:(k + 1) * LANES] = r
    wi_o[...] = mm("wi") * (1.0 / 16.0)
    for k, r in enumerate(roped("qc", c8, s8, 4, 32, DIFF_QK_DIM ** -0.5)):
        qc_o[:, k * LANES:(k + 1) * LANES] = r
    for k, r in enumerate(roped("kc", c8, s8, 4, 32, 1.0)):
        kc_o[:, k * LANES:(k + 1) * LANES] = r
    vc_o[...] = mm("vc").astype(BF16)
    qm_o[...] = (mm("qm") * (HEAD_DIM ** -0.5)).astype(BF16)


def _proj(x2, g, w, bf, tabs, consts, *, seq, tm):
    n, d = x2.shape
    grid = (n // tm,)
    row = lambda i: (i, 0)
    fixed2 = lambda i: (0, 0)
    fixed3 = lambda i: (0, 0, 0)
    out_w = (("qa", 512, BF16), ("ka", 512, BF16), ("va", 256, BF16), ("qb", 256, BF16),
             ("kb", 128, BF16), ("vb", 128, BF16), ("qi", 256, BF16), ("ki", 256, BF16),
             ("wi", 128, F32), ("qc", 256, BF16), ("kc", 256, BF16), ("vc", 256, BF16),
             ("qm", 256, BF16))
    eq, ek, oq, ok = consts
    outs = pl.pallas_call(
        functools.partial(_proj_kernel, tiles_per_batch=seq // tm),
        grid=grid,
        in_specs=[pl.BlockSpec((tm, d), row), pl.BlockSpec((1, d), fixed2),
                  pl.BlockSpec((d, PROJ_COLS), fixed2), pl.BlockSpec((1, LANES), fixed2),
                  pl.BlockSpec((tm, LANES), row), pl.BlockSpec((tm, LANES), row),
                  pl.BlockSpec((tm, LANES), row), pl.BlockSpec((tm, LANES), row),
                  pl.BlockSpec((3, LANES, 512), fixed3), pl.BlockSpec((3, LANES, 512), fixed3),
                  pl.BlockSpec((1, 512), fixed2), pl.BlockSpec((1, 512), fixed2)],
        out_specs=[pl.BlockSpec((tm, wd), row) for _, wd, _ in out_w],
        out_shape=[jax.ShapeDtypeStruct((n, wd), dt) for _, wd, dt in out_w],
        scratch_shapes=[pltpu.VMEM((1, LANES), F32)],
        compiler_params=_cparams(("arbitrary",)),
        name="proj",
    )(x2, g, w, bf, *tabs, eq, ek, oq, ok)
    return dict(zip([nm for nm, _, _ in out_w], outs))


def _softmax_init(m_ref, l_ref, acc_ref):
    m_ref[...] = jnp.full(m_ref.shape, NEG, F32)
    l_ref[...] = jnp.zeros(l_ref.shape, F32)
    acc_ref[...] = jnp.zeros(acc_ref.shape, F32)


def _softmax_step(s, v, idx, m_ref, l_ref, acc_ref):
    m_prev = m_ref[idx]
    m_new = jnp.maximum(m_prev, jnp.max(s, axis=-1, keepdims=True))
    alpha = jnp.exp(m_prev - m_new)
    p = jnp.exp(s - m_new)
    l_ref[idx] = alpha * l_ref[idx] + jnp.sum(p, axis=-1, keepdims=True)
    acc_ref[idx] = alpha * acc_ref[idx] + _dot(p.astype(BF16), v)
    m_ref[idx] = m_new


def _lane_half_mask(rows):
    return lax.broadcasted_iota(I32, (rows, LANES), 1) < HEAD_DIM


def _causal(tq, tk, row0, col0):
    row = row0 + lax.broadcasted_iota(I32, (tq, tk), 0)
    col = col0 + lax.broadcasted_iota(I32, (tq, tk), 1)
    return col <= row


def _fox_kernel(q_ref, k_ref, v_ref, o_ref, m_ref, l_ref, acc_ref, *, tq):
    i = pl.program_id(1)
    _softmax_init(m_ref, l_ref, acc_ref)

    def chunk(j, masked):
        start = pl.multiple_of(j * tq, tq)
        for h in range(N_HEADS):
            q = q_ref[:, h * LANES:(h + 1) * LANES]
            k = k_ref[pl.ds(start, tq), h * LANES:(h + 1) * LANES]
            s = _dot_t(q, k)
            if masked:
                s = jnp.where(_causal(tq, tq, 0, 0), s, NEG)
            g = h // 2
            v = v_ref[pl.ds(start, tq), g * LANES:(g + 1) * LANES]
            _softmax_step(s, v, h, m_ref, l_ref, acc_ref)

    def body(j, c):
        chunk(j, False)
        return c

    lax.fori_loop(0, i, body, 0)
    chunk(i, True)

    half = _lane_half_mask(tq)
    for g in range(N_HEADS // 2):
        o0 = acc_ref[2 * g] / l_ref[2 * g]
        o1 = acc_ref[2 * g + 1] / l_ref[2 * g + 1]
        o_ref[:, g * LANES:(g + 1) * LANES] = jnp.where(half, o0, o1).astype(o_ref.dtype)


def _fox(qa, ka, va, *, batch, seq, tq):
    n = batch * seq
    nq = seq // tq
    return pl.pallas_call(
        functools.partial(_fox_kernel, tq=tq),
        grid=(batch, nq),
        in_specs=[pl.BlockSpec((tq, 512), lambda b, i: (b * nq + i, 0)),
                  pl.BlockSpec((seq, 512), lambda b, i: (b, 0)),
                  pl.BlockSpec((seq, 256), lambda b, i: (b, 0))],
        out_specs=pl.BlockSpec((tq, 256), lambda b, i: (b * nq + i, 0)),
        out_shape=jax.ShapeDtypeStruct((n, 256), BF16),
        scratch_shapes=[pltpu.VMEM((N_HEADS, tq, 1), F32), pltpu.VMEM((N_HEADS, tq, 1), F32),
                        pltpu.VMEM((N_HEADS, tq, LANES), F32)],
        compiler_params=_cparams(("arbitrary", "arbitrary")),
        name="fox",
    )(qa, ka, va)


def _diff_kernel(q_ref, k_ref, v_ref, dl_ref, li_ref, g_ref, o_ref, qm_ref, m_ref, l_ref, acc_ref, *, tq):
    i = pl.program_id(1)
    _softmax_init(m_ref, l_ref, acc_ref)
    lane = lax.broadcasted_iota(I32, (tq, LANES), 1)
    for h in range(N_HEADS):
        g = h // 2
        qg = q_ref[:, g * LANES:(g + 1) * LANES]
        for m in range(2):
            lo = (h % 2) * HEAD_DIM + m * DIFF_QK_DIM
            sel = (lane >= lo) & (lane < lo + DIFF_QK_DIM)
            qm_ref[2 * h + m] = jnp.where(sel, qg, jnp.zeros_like(qg))

    def chunk(j, masked):
        start = pl.multiple_of(j * tq, tq)
        for h in range(N_HEADS):
            g = h // 2
            k = k_ref[pl.ds(start, tq), g * LANES:(g + 1) * LANES]
            v = v_ref[pl.ds(start, tq), g * LANES:(g + 1) * LANES]
            for m in range(2):
                s = _dot_t(qm_ref[2 * h + m], k)
                if masked:
                    s = jnp.where(_causal(tq, tq, 0, 0), s, NEG)
                _softmax_step(s, v, 2 * h + m, m_ref, l_ref, acc_ref)

    def body(j, c):
        chunk(j, False)
        return c

    lax.fori_loop(0, i, body, 0)
    chunk(i, True)

    dl = dl_ref[...]
    lam_init = li_ref[...]
    lam = (jnp.exp(jnp.sum(dl[0:1] * dl[1:2], axis=-1, keepdims=True))
           - jnp.exp(jnp.sum(dl[2:3] * dl[3:4], axis=-1, keepdims=True)) + lam_init)
    half = _lane_half_mask(tq)
    for g in range(N_HEADS // 2):
        outs = []
        for h in (2 * g, 2 * g + 1):
            a1 = acc_ref[2 * h] / l_ref[2 * h]
            a2 = acc_ref[2 * h + 1] / l_ref[2 * h + 1]
            outs.append(a1 - lam * a2)
        o = jnp.where(half, outs[0], outs[1])
        sq = o * o
        s_lo = jnp.sum(jnp.where(half, sq, 0.0), axis=-1, keepdims=True)
        s_hi = jnp.sum(sq, axis=-1, keepdims=True) - s_lo
        r = jnp.where(half, lax.rsqrt(s_lo * (1.0 / HEAD_DIM) + NORM_EPS),
                      lax.rsqrt(s_hi * (1.0 / HEAD_DIM) + NORM_EPS))
        o = (o * r * g_ref[...]) * (1.0 - lam_init)
        o_ref[:, g * LANES:(g + 1) * LANES] = o.astype(o_ref.dtype)


def _diff(qc, kc, vc, dl, lam_init, g2, *, batch, seq, tq):
    n = batch * seq
    nq = seq // tq
    fixed = lambda b, i: (0, 0)
    return pl.pallas_call(
        functools.partial(_diff_kernel, tq=tq),
        grid=(batch, nq),
        in_specs=[pl.BlockSpec((tq, 256), lambda b, i: (b * nq + i, 0)),
                  pl.BlockSpec((seq, 256), lambda b, i: (b, 0)),
                  pl.BlockSpec((seq, 256), lambda b, i: (b, 0)),
                  pl.BlockSpec((4, DIFF_QK_DIM), fixed), pl.BlockSpec((1, 1), fixed),
                  pl.BlockSpec((1, LANES), fixed)],
        out_specs=pl.BlockSpec((tq, 256), lambda b, i: (b * nq + i, 0)),
        out_shape=jax.ShapeDtypeStruct((n, 256), BF16),
        scratch_shapes=[pltpu.VMEM((2 * N_HEADS, tq, LANES), BF16),
                        pltpu.VMEM((2 * N_HEADS, tq, 1), F32), pltpu.VMEM((2 * N_HEADS, tq, 1), F32),
                        pltpu.VMEM((2 * N_HEADS, tq, LANES), F32)],
        compiler_params=_cparams(("arbitrary", "arbitrary")),
        name="diff",
    )(qc, kc, vc, dl, lam_init, g2)


def _mem_kernel(q_ref, mem_ref, g_ref, w_ref, o_ref, k_s, v_s, *, tq):
    i = pl.program_id(1)

    @pl.when(i == 0)
    def _():
        hm = _rms(mem_ref[0], g_ref[...]).astype(BF16)
        kv = _dot(hm, w_ref[...])
        k_s[...] = kv[:, :256].astype(BF16)
        v_s[...] = kv[:, 256:].astype(BF16)

    lane = lax.broadcasted_iota(I32, (tq, LANES), 1)
    half = lane < HEAD_DIM
    for g in range(MEM_HEADS // 2):
        qg = q_ref[:, g * LANES:(g + 1) * LANES]
        k = k_s[:, g * LANES:(g + 1) * LANES]
        v = v_s[:, g * LANES:(g + 1) * LANES]
        outs = []
        for hh in range(2):
            q = jnp.where(half if hh == 0 else jnp.logical_not(half), qg, jnp.zeros_like(qg))
            s = _dot_t(q, k)
            m = jnp.max(s, axis=-1, keepdims=True)
            p = jnp.exp(s - m)
            l = jnp.sum(p, axis=-1, keepdims=True)
            outs.append(_dot(p.astype(BF16), v) / l)
        o_ref[:, g * LANES:(g + 1) * LANES] = jnp.where(half, outs[0], outs[1]).astype(o_ref.dtype)


def _mem_attn(qm, mem, g, w_kv, *, batch, seq, tq):
    n = batch * seq
    nq = seq // tq
    m_len, d = mem.shape[1], mem.shape[2]
    fixed = lambda b, i: (0, 0)
    return pl.pallas_call(
        functools.partial(_mem_kernel, tq=tq),
        grid=(batch, nq),
        in_specs=[pl.BlockSpec((tq, 256), lambda b, i: (b * nq + i, 0)),
                  pl.BlockSpec((1, m_len, d), lambda b, i: (b, 0, 0)),
                  pl.BlockSpec((1, d), fixed), pl.BlockSpec((d, 512), fixed)],
        out_specs=pl.BlockSpec((tq, 256), lambda b, i: (b * nq + i, 0)),
        out_shape=jax.ShapeDtypeStruct((n, 256), BF16),
        scratch_shapes=[pltpu.VMEM((m_len, 256), BF16), pltpu.VMEM((m_len, 256), BF16)],
        compiler_params=_cparams(("arbitrary", "arbitrary")),
        name="mem_attn",
    )(qm, mem, g, w_kv)


def _dsa_kernel(qi_ref, wi_ref, qb_ref, ki_ref, kb_ref, vb_ref, o_ref,
                keys_ref, m_ref, l_ref, acc_ref, *, tq, tk, top_k, idx_bits):
    i = pl.program_id(1)
    row0 = i * tq
    n_ch = (row0 + tq - 1) // tk + 1
    n_sub = tk // LANES

    qi = qi_ref[...]
    wi = wi_ref[...]
    lane_q = lax.broadcasted_iota(I32, qi.shape, 1)

    def score_chunk(j, c):
        start = pl.multiple_of(j * tk, tk)
        kc = ki_ref[pl.ds(start, tk), :]
        score = jnp.zeros((tq, tk), F32)
        for h in range(IDX_HEADS):
            qh = jnp.where((lane_q >> 5) == h, qi, jnp.zeros_like(qi))
            s = _dot_t(qh, kc)
            score = score + jnp.maximum(s, 0.0) * wi[:, h:h + 1]
        bits = lax.bitcast_convert_type(score, I32)
        key = bits ^ ((bits >> 31) & jnp.int32(0x7FFFFFFF))
        keys_ref[j] = jnp.where(_causal(tq, tk, row0, j * tk), key, jnp.int32(INT_MIN))
        return c

    lax.fori_loop(0, n_ch, score_chunk, 0)

    def count(hits):
        def body(j, acc):
            hit = hits(keys_ref[j], j)
            part = hit[:, 0:LANES]
            for c in range(1, n_sub):
                part = part + hit[:, c * LANES:(c + 1) * LANES]
            return acc + part
        acc = lax.fori_loop(0, n_ch, body, jnp.zeros((tq, LANES), F32))
        return jnp.sum(acc, axis=-1, keepdims=True)

    def bit_step(b, thr):
        cand = thr + (jnp.int32(1) << (31 - b))
        cnt = count(lambda kc, j: jnp.where(kc >= cand, 1.0, 0.0))
        return jnp.where(cnt >= top_k, cand, thr)

    thr = lax.fori_loop(0, 32, bit_step, jnp.full((tq, 1), INT_MIN, I32))

    n_ge = count(lambda kc, j: jnp.where(kc >= thr, 1.0, 0.0))
    n_gt = count(lambda kc, j: jnp.where(kc > thr, 1.0, 0.0))
    need = top_k - n_gt
    tie = jnp.where(thr > INT_MIN, jnp.where(n_ge > top_k, 1.0, 0.0), 0.0)

    @pl.when(jnp.max(tie) > 0.0)
    def _():
        def col_of(j):
            return j * tk + lax.broadcasted_iota(I32, (tq, tk), 1)

        def idx_step(b, cut):
            cand = cut + (jnp.int32(1) << (idx_bits - 1 - b))
            cnt = count(lambda kc, j: jnp.where(kc == thr, jnp.where(col_of(j) < cand, 1.0, 0.0), 0.0))
            return jnp.where(cnt < need, cand, cut)

        cut = lax.fori_loop(0, idx_bits, idx_step, jnp.zeros((tq, 1), I32))
        cut = jnp.where(tie > 0.0, cut, jnp.int32(2 ** 30))

        def demote(j, c):
            kc = keys_ref[j]
            demoted = jnp.where(col_of(j) > cut, kc - 1, kc)
            keys_ref[j] = jnp.where(kc == thr, demoted, kc)
            return c

        lax.fori_loop(0, n_ch, demote, 0)

    thr_sel = jnp.maximum(thr, jnp.int32(INT_MIN + 1))

    _softmax_init(m_ref, l_ref, acc_ref)
    lane = lax.broadcasted_iota(I32, (tq, LANES), 1)
    half = lane < HEAD_DIM
    qs = []
    for h in range(N_HEADS):
        g = h // 2
        qg = qb_ref[:, g * LANES:(g + 1) * LANES]
        qs.append(jnp.where(half if h % 2 == 0 else jnp.logical_not(half), qg, jnp.zeros_like(qg)))

    def attn_chunk(j, c):
        start = pl.multiple_of(j * tk, tk)
        sel = keys_ref[j] >= thr_sel
        k = kb_ref[pl.ds(start, tk), :]
        v = vb_ref[pl.ds(start, tk), :]
        for h in range(N_HEADS):
            s = jnp.where(sel, _dot_t(qs[h], k), NEG)
            _softmax_step(s, v, h, m_ref, l_ref, acc_ref)
        return c

    lax.fori_loop(0, n_ch, attn_chunk, 0)

    for g in range(N_HEADS // 2):
        o0 = acc_ref[2 * g] / l_ref[2 * g]
        o1 = acc_ref[2 * g + 1] / l_ref[2 * g + 1]
        o_ref[:, g * LANES:(g + 1) * LANES] = jnp.where(half, o0, o1).astype(o_ref.dtype)


def _dsa(qi, wi, qb, ki, kb, vb, *, batch, seq, tq, tk):
    n = batch * seq
    nq = seq // tq
    top_k = min(IDX_TOPK_MAX, seq // 4)
    idx_bits = max(1, int(math.ceil(math.log2(seq))))
    qrow = lambda b, i: (b * nq + i, 0)
    kv = lambda b, i: (b, 0)
    return pl.pallas_call(
        functools.partial(_dsa_kernel, tq=tq, tk=tk, top_k=top_k, idx_bits=idx_bits),
        grid=(batch, nq),
        in_specs=[pl.BlockSpec((tq, 256), qrow), pl.BlockSpec((tq, LANES), qrow),
                  pl.BlockSpec((tq, 256), qrow),
                  pl.BlockSpec((seq, 256), kv), pl.BlockSpec((seq, LANES), kv),
                  pl.BlockSpec((seq, LANES), kv)],
        out_specs=pl.BlockSpec((tq, 256), qrow),
        out_shape=jax.ShapeDtypeStruct((n, 256), BF16),
        scratch_shapes=[pltpu.VMEM((seq // tk, tq, tk), I32),
                        pltpu.VMEM((N_HEADS, tq, 1), F32), pltpu.VMEM((N_HEADS, tq, 1), F32),
                        pltpu.VMEM((N_HEADS, tq, LANES), F32)],
        compiler_params=_cparams(("arbitrary", "arbitrary")),
        name="dsa",
    )(qi, wi, qb, ki, kb, vb)


def _merge_kernel(x_ref, g_ref, oa_ref, ob_ref, oc_ref, om_ref, wg_ref, bg_ref, wb_ref, wo_ref, y_ref):
    x = x_ref[...]
    d = x.shape[1]
    h = _rms(x, g_ref[...]).astype(BF16)
    acc = jnp.zeros(x.shape, F32)
    for nb, o_ref in enumerate((oa_ref, ob_ref, oc_ref, om_ref)):
        gate = jax.nn.sigmoid(_dot(h, wg_ref[:, nb * d:(nb + 1) * d]) + bg_ref[:, nb * d:(nb + 1) * d])
        acc = acc + gate * _dot(o_ref[...], wb_ref[nb])
    y_ref[...] = x + _dot(acc.astype(BF16), wo_ref[...])


def _merge(x2, g, oa, ob, oc, om, wg, bg, wb, wo, *, tm):
    n, d = x2.shape
    row = lambda i: (i, 0)
    fixed2 = lambda i: (0, 0)
    bw = oa.shape[1]
    return pl.pallas_call(
        _merge_kernel,
        grid=(n // tm,),
        in_specs=[pl.BlockSpec((tm, d), row), pl.BlockSpec((1, d), fixed2),
                  pl.BlockSpec((tm, bw), row), pl.BlockSpec((tm, bw), row),
                  pl.BlockSpec((tm, bw), row), pl.BlockSpec((tm, bw), row),
                  pl.BlockSpec((d, N_BRANCH * d), fixed2), pl.BlockSpec((1, N_BRANCH * d), fixed2),
                  pl.BlockSpec((N_BRANCH, bw, d), lambda i: (0, 0, 0)), pl.BlockSpec((d, d), fixed2)],
        out_specs=pl.BlockSpec((tm, d), row),
        out_shape=jax.ShapeDtypeStruct((n, d), F32),
        compiler_params=_cparams(("arbitrary",)),
        name="merge",
    )(x2, g, oa, ob, oc, om, wg, bg, wb, wo)


def _ffn_kernel(x_ref, g_ref, wi_ref, wo_ref, gf_ref, y_ref, *, hidden, chunk, final):
    x = x_ref[...]
    h = _rms(x, g_ref[...]).astype(BF16)
    acc = jnp.zeros(x.shape, F32)
    for c in range(hidden // chunk):
        a = _dot(h, wi_ref[:, c * chunk:(c + 1) * chunk])
        b = _dot(h, wi_ref[:, hidden + c * chunk:hidden + (c + 1) * chunk])
        t = (a * jax.nn.sigmoid(a)) * b
        acc = acc + _dot(t.astype(BF16), wo_ref[c * chunk:(c + 1) * chunk, :])
    y = x + acc
    if final:
        y = _rms(y, gf_ref[...])
    y_ref[...] = y


def _ffn(x2, g, w_in, w_out, g_final, *, tm, final):
    n, d = x2.shape
    hidden = w_out.shape[0]
    chunk = hidden // 4 if (hidden // 4) % LANES == 0 else hidden
    row = lambda i: (i, 0)
    fixed2 = lambda i: (0, 0)
    return pl.pallas_call(
        functools.partial(_ffn_kernel, hidden=hidden, chunk=chunk, final=final),
        grid=(n // tm,),
        in_specs=[pl.BlockSpec((tm, d), row), pl.BlockSpec((1, d), fixed2),
                  pl.BlockSpec((d, 2 * hidden), fixed2), pl.BlockSpec((hidden, d), fixed2),
                  pl.BlockSpec((1, d), fixed2)],
        out_specs=pl.BlockSpec((tm, d), row),
        out_shape=jax.ShapeDtypeStruct((n, d), F32),
        compiler_params=_cparams(("arbitrary",)),
        name="ffn",
    )(x2, g, w_in, w_out, g_final)


_IN_SPLITS = (256, 256, 256, 4, 256, 64, 64, 256, 32, 8, 256, 256, 256, 256)


def _pack_w_in(w):
    d = w.shape[0]
    offs = np.cumsum((0,) + _IN_SPLITS)
    qa, ka, va, fa, qb, kb, vb, qi, ki, wi, qc, kc, vc, qm = (
        w[:, offs[k]:offs[k + 1]] for k in range(len(_IN_SPLITS)))

    def pad_heads(a):
        a = a.reshape(d, N_HEADS, HEAD_DIM)
        return jnp.pad(a, ((0, 0), (0, 0), (0, LANES - HEAD_DIM))).reshape(d, N_HEADS * LANES)

    def pad_to(a, wd):
        return jnp.pad(a, ((0, 0), (0, wd - a.shape[1])))

    cols = [pad_heads(qa), pad_heads(ka), va, pad_to(fa, LANES), qb,
            jnp.concatenate([kb, kb], axis=1), jnp.concatenate([vb, vb], axis=1),
            qi, jnp.tile(ki, (1, IDX_HEADS)), pad_to(wi, LANES), qc, kc, vc, qm]
    return jnp.concatenate(cols, axis=1).astype(BF16)


def _fox_consts():
    eq = np.zeros((3, LANES, 512), np.float32)
    ek = np.zeros((3, LANES, 512), np.float32)
    oq = np.zeros((1, 512), np.float32)
    ok = np.zeros((1, 512), np.float32)
    for h in range(N_HEADS):
        base = h * LANES + HEAD_DIM
        for t in range(3):
            eq[t, h, base + t] = 1.0
            ek[t, h, base + 3 + t] = -1.0
            oq[0, base + 3 + t] = 1.0
            ok[0, base + t] = 1.0
    return (jnp.asarray(eq, BF16), jnp.asarray(ek, BF16), jnp.asarray(oq), jnp.asarray(ok))


def _rope_tables(positions, rot_dim, period):
    half = rot_dim // 2
    inv_freq = ROPE_THETA ** (-jnp.arange(0, rot_dim, 2, dtype=F32) / rot_dim)
    ang = positions.astype(F32).reshape(-1, 1) * inv_freq
    cos, sin = jnp.cos(ang), jnp.sin(ang)
    n = ang.shape[0]
    c = jnp.concatenate([cos, cos, jnp.ones((n, period - 2 * half), F32)], axis=1)
    s = jnp.concatenate([-sin, sin, jnp.zeros((n, period - 2 * half), F32)], axis=1)
    reps = LANES // period
    return jnp.tile(c, (1, reps)), jnp.tile(s, (1, reps))


def kernel(x, mem, positions, norm_mix_g, w_in, b_forget, diff_lambda, diff_subln_g, norm_mem_g,
           w_mem_kv, w_branch, w_gate, b_gate, w_out, norm_ffn_g, w_ffn_in, w_ffn_out, final_norm_g):
    batch, seq, d = x.shape
    depth = w_in.shape[0]
    n = batch * seq
    tm = min(256, seq)
    tq = min(256, seq)
    tq_dsa = min(128, seq)
    tk_dsa = min(512, seq)

    c16, s16 = _rope_tables(positions, HEAD_DIM // 4, HEAD_DIM)
    c8, s8 = _rope_tables(positions, DIFF_QK_DIM // 4, DIFF_QK_DIM)
    consts = _fox_consts()

    x2 = x.reshape(n, d)
    for l in range(depth):
        lam_init = 0.8 - 0.6 * math.exp(-0.3 * l)
        bf = jnp.pad(b_forget[l], (0, LANES - b_forget.shape[1])).reshape(1, LANES)
        z = _proj(x2, norm_mix_g[l].reshape(1, d), _pack_w_in(w_in[l]), bf, (c16, s16, c8, s8), consts,
                  seq=seq, tm=tm)
        o_a = _fox(z["qa"], z["ka"], z["va"], batch=batch, seq=seq, tq=tq)
        o_b = _dsa(z["qi"], z["wi"], z["qb"], z["ki"], z["kb"], z["vb"],
                   batch=batch, seq=seq, tq=tq_dsa, tk=tk_dsa)
        o_c = _diff(z["qc"], z["kc"], z["vc"], diff_lambda[l], jnp.full((1, 1), lam_init, F32),
                    jnp.tile(diff_subln_g[l], 2).reshape(1, LANES), batch=batch, seq=seq, tq=tq)
        o_m = _mem_attn(z["qm"], mem, norm_mem_g[l].reshape(1, d), w_mem_kv[l].astype(BF16),
                        batch=batch, seq=seq, tq=tq)
        x2 = _merge(x2, norm_mix_g[l].reshape(1, d), o_a, o_b, o_c, o_m,
                    w_gate[l].astype(BF16), b_gate[l].reshape(1, -1), w_branch[l].astype(BF16),
                    w_out[l].astype(BF16), tm=tm)
        x2 = _ffn(x2, norm_ffn_g[l].reshape(1, d), w_ffn_in[l].astype(BF16), w_ffn_out[l].astype(BF16),
                  final_norm_g.reshape(1, d), tm=tm, final=(l == depth - 1))
    return x2.reshape(batch, seq, d)
```

```python
import functools
import math

import numpy as np
import jax
import jax.numpy as jnp
from jax import lax
from jax.experimental import pallas as pl
from jax.experimental.pallas import tpu as pltpu

F32 = jnp.float32
BF16 = jnp.bfloat16
I32 = jnp.int32

HEAD_DIM = 64
N_BRANCH = 4
N_HEADS = 4
IDX_HEADS = 8
IDX_DIM = 32
IDX_TOPK_MAX = 256
DIFF_QK_DIM = 32
ROPE_THETA = 500000.0
NORM_EPS = 1e-6
LANES = 128
BF16_ROWS = 16
NEG = -1e30
INT_MIN = -2 ** 31
TILE = 256
VT_ROWS = HEAD_DIM + BF16_ROWS
BITS_PER_CHECK = 4

VMEM_LIMIT = 56 * 1024 * 1024

_SEG = {}
_off = 0
for _name, _w in (("qa", 512), ("ka", 512), ("fa", 128), ("qb", 256), ("kb", 128),
                  ("qi", 256), ("ki", 256), ("qc", 256), ("kc", 256), ("qm", 256)):
    _SEG[_name] = (_off, _off + _w)
    _off += _w
PROJ_COLS = _off
_TSEG = {"va": (0, 256), "vc": (256, 512), "vb": (512, 576), "wi": (576, 592)}
PROJ_T_ROWS = 592

_DN_T = (((1,), (1,)), ((), ()))


def _dot(a, b):
    return jnp.dot(a, b, preferred_element_type=F32)


def _dot_t(a, b):
    return lax.dot_general(a, b, _DN_T, preferred_element_type=F32)


def _rms(x, g):
    return x * lax.rsqrt(jnp.mean(x * x, axis=-1, keepdims=True) + NORM_EPS) * g


def _cparams(sem):
    return pltpu.CompilerParams(dimension_semantics=sem, vmem_limit_bytes=VMEM_LIMIT)


def _split3(x):
    hi = x.astype(BF16)
    r = x - hi.astype(F32)
    mid = r.astype(BF16)
    lo = (r - mid.astype(F32)).astype(BF16)
    return hi, mid, lo


def _ones_rows(cols):
    r = lax.broadcasted_iota(I32, (BF16_ROWS, cols), 0)
    return jnp.where(r == 0, 1.0, 0.0).astype(BF16)


def _rope(z, c, s, half, period):
    lane = lax.broadcasted_iota(I32, z.shape, 1)
    up = pltpu.roll(z, LANES - half, 1)
    dn = pltpu.roll(z, half, 1)
    partner = jnp.where((lane & (period - 1)) < half, up, dn)
    return z * c + partner * s


def _proj_kernel(x_ref, g_ref, w_ref, wt_ref, bf_ref, c16_ref, s16_ref, c8_ref, s8_ref, eq_ref, ek_ref,
                 oq_ref, ok_ref,
                 qa_o, ka_o, qb_o, kb_o, qi_o, ki_o, qc_o, kc_o, qm_o, vat_o, vct_o, vbt_o, wit_o,
                 carry_ref, *, tiles_per_batch):
    i = pl.program_id(0)
    tm = x_ref.shape[0]
    h = _rms(x_ref[...], g_ref[...]).astype(BF16)

    def mm(name):
        lo, hi = _SEG[name]
        return _dot(h, w_ref[:, lo:hi])

    f = mm("fa") + bf_ref[...]
    ls = jnp.minimum(f, 0.0) - jnp.log(1.0 + jnp.exp(-jnp.abs(f)))
    r_i = lax.broadcasted_iota(I32, (tm, tm), 0)
    c_i = lax.broadcasted_iota(I32, (tm, tm), 1)
    ltri = jnp.where(c_i <= r_i, 1.0, 0.0).astype(BF16)
    p_hi, p_mid, p_lo = _split3(ls)
    pre = _dot(ltri, p_hi) + _dot(ltri, p_mid) + _dot(ltri, p_lo)

    @pl.when(i % tiles_per_batch == 0)
    def _():
        carry_ref[...] = jnp.zeros_like(carry_ref)

    cum = pre + carry_ref[...]
    carry_ref[...] = cum[tm - 1:tm, :]
    c_hi, c_mid, c_lo = _split3(cum)

    qa = mm("qa") * (HEAD_DIM ** -0.5)
    qa = qa + _dot(c_hi, eq_ref[0]) + _dot(c_mid, eq_ref[1]) + _dot(c_lo, eq_ref[2]) + oq_ref[...]
    qa_o[...] = qa.astype(BF16)
    ka = mm("ka")
    ka = ka + _dot(c_hi, ek_ref[0]) + _dot(c_mid, ek_ref[1]) + _dot(c_lo, ek_ref[2]) + ok_ref[...]
    ka_o[...] = ka.astype(BF16)

    c16 = c16_ref[...]
    s16 = s16_ref[...]
    c8 = c8_ref[...]
    s8 = s8_ref[...]

    def roped(name, out_ref, c, s, half, period, scale):
        z = mm(name)
        for k in range(z.shape[1] // LANES):
            r = _rope(z[:, k * LANES:(k + 1) * LANES], c, s, half, period)
            if scale != 1.0:
                r = r * scale
            out_ref[:, k * LANES:(k + 1) * LANES] = r.astype(BF16)

    roped("qb", qb_o, c16, s16, 8, 64, HEAD_DIM ** -0.5)
    roped("kb", kb_o, c16, s16, 8, 64, 1.0)
    roped("qi", qi_o, c8, s8, 4, 32, 1.0)
    roped("ki", ki_o, c8, s8, 4, 32, 1.0)
    roped("qc", qc_o, c8, s8, 4, 32, DIFF_QK_DIM ** -0.5)
    roped("kc", kc_o, c8, s8, 4, 32, 1.0)
    qm_o[...] = (mm("qm") * (HEAD_DIM ** -0.5)).astype(BF16)

    zt = _dot_t(wt_ref[...], h)
    ones = _ones_rows(tm)
    for name, out_ref, heads in (("va", vat_o, N_HEADS), ("vc", vct_o, N_HEADS), ("vb", vbt_o, 1)):
        lo = _TSEG[name][0]
        for hd in range(heads):
            out_ref[0, hd * VT_ROWS:hd * VT_ROWS + HEAD_DIM, :] = (
                zt[lo + hd * HEAD_DIM:lo + (hd + 1) * HEAD_DIM, :].astype(BF16))
            out_ref[0, hd * VT_ROWS + HEAD_DIM:(hd + 1) * VT_ROWS, :] = ones
    lo, hi = _TSEG["wi"]
    wit_o[...] = zt[lo:hi, :] * (1.0 / 16.0)


def _proj(x2, g, w, wt, bf, tabs, consts, *, seq):
    n, d = x2.shape
    tm = TILE
    grid = (n // tm,)
    row = lambda i: (i, 0)
    fixed2 = lambda i: (0, 0)
    fixed3 = lambda i: (0, 0, 0)
    chunk3 = lambda i: (i, 0, 0)
    eq, ek, oq, ok = consts
    names = ("qa", "ka", "qb", "kb", "qi", "ki", "qc", "kc", "qm")
    out_specs = [pl.BlockSpec((tm, _SEG[nm][1] - _SEG[nm][0]), row) for nm in names]
    out_shape = [jax.ShapeDtypeStruct((n, _SEG[nm][1] - _SEG[nm][0]), BF16) for nm in names]
    for heads in (N_HEADS, N_HEADS, 1):
        out_specs.append(pl.BlockSpec((1, heads * VT_ROWS, tm), chunk3))
        out_shape.append(jax.ShapeDtypeStruct((n // tm, heads * VT_ROWS, tm), BF16))
    out_specs.append(pl.BlockSpec((BF16_ROWS, tm), lambda i: (0, i)))
    out_shape.append(jax.ShapeDtypeStruct((BF16_ROWS, n), F32))
    outs = pl.pallas_call(
        functools.partial(_proj_kernel, tiles_per_batch=seq // tm),
        grid=grid,
        in_specs=[pl.BlockSpec((tm, d), row), pl.BlockSpec((1, d), fixed2),
                  pl.BlockSpec((d, PROJ_COLS), fixed2), pl.BlockSpec((PROJ_T_ROWS, d), fixed2),
                  pl.BlockSpec((1, LANES), fixed2),
                  pl.BlockSpec((tm, LANES), row), pl.BlockSpec((tm, LANES), row),
                  pl.BlockSpec((tm, LANES), row), pl.BlockSpec((tm, LANES), row),
                  pl.BlockSpec((3, LANES, 512), fixed3), pl.BlockSpec((3, LANES, 512), fixed3),
                  pl.BlockSpec((1, 512), fixed2), pl.BlockSpec((1, 512), fixed2)],
        out_specs=out_specs,
        out_shape=out_shape,
        scratch_shapes=[pltpu.VMEM((1, LANES), F32)],
        compiler_params=_cparams(("arbitrary",)),
        name="proj",
    )(x2, g, w, wt, bf, *tabs, eq, ek, oq, ok)
    return dict(zip(names + ("vaT", "vcT", "vbT", "wiT"), outs))


def _softmax_init(m_ref, acc_ref):
    m_ref[...] = jnp.full(m_ref.shape, NEG, F32)
    acc_ref[...] = jnp.zeros(acc_ref.shape, F32)


def _softmax_steps(streams, m_ref, acc_ref):
    alphas, ps = [], []
    for idx, sts, _ in streams:
        m_prev = m_ref[idx]
        m_new = m_prev
        for st in sts:
            m_new = jnp.maximum(m_new, jnp.max(st, axis=0, keepdims=True))
        alphas.append(jnp.exp(m_prev - m_new))
        ps.append([jnp.exp(st - m_new).astype(BF16) for st in sts])
        m_ref[idx] = m_new
    pvs = []
    for (_, _, vts), p_list in zip(streams, ps):
        pv = _dot(vts[0], p_list[0])
        for vt, p in zip(vts[1:], p_list[1:]):
            pv = pv + _dot(vt, p)
        pvs.append(pv)
    for (idx, _, _), alpha, pv in zip(streams, alphas, pvs):
        acc_ref[idx] = alpha * acc_ref[idx] + pv


def _chunk_pairs(n_full, step):
    def body(jj, c):
        step([2 * jj, 2 * jj + 1])
        return c

    lax.fori_loop(0, n_full // 2, body, 0)

    @pl.when(n_full % 2 == 1)
    def _():
        step([n_full - 1])


def _normalized(acc_ref, idx):
    acc = acc_ref[idx]
    return acc[0:HEAD_DIM, :] / acc[HEAD_DIM:HEAD_DIM + 1, :]


def _store_pair(o_ref, g, o0, o1, scale_row=None):
    o = jnp.concatenate([o0, o1], axis=0).T
    if scale_row is not None:
        o = o * scale_row
    o_ref[:, g * LANES:(g + 1) * LANES] = o.astype(o_ref.dtype)


def _diag_mask(t):
    key = lax.broadcasted_iota(I32, (t, t), 0)
    qry = lax.broadcasted_iota(I32, (t, t), 1)
    return key <= qry


def _masked_heads(q_ref, qm_ref, width):
    tq = q_ref.shape[0]
    lane = lax.broadcasted_iota(I32, (tq, LANES), 1)
    per_group = LANES // width
    for idx in range(2 * per_group):
        g = idx // per_group
        qg = q_ref[:, g * LANES:(g + 1) * LANES]
        lo = (idx % per_group) * width
        sel = (lane >= lo) & (lane < lo + width)
        qm_ref[idx] = jnp.where(sel, qg, jnp.zeros_like(qg))


def _fox_kernel(q_ref, k_ref, vt_ref, o_ref, m_ref, acc_ref):
    i = pl.program_id(1)
    t = q_ref.shape[0]
    _softmax_init(m_ref, acc_ref)

    def step(js, masked=False):
        streams = []
        for h in range(N_HEADS):
            q = q_ref[:, h * LANES:(h + 1) * LANES]
            sts = [_dot_t(k_ref[pl.ds(pl.multiple_of(j * t, t), t), h * LANES:(h + 1) * LANES], q) for j in js]
            if masked:
                sts = [jnp.where(_diag_mask(t), st, NEG) for st in sts]
            streams.append((h, sts, [vt_ref[j, h * VT_ROWS:(h + 1) * VT_ROWS, :] for j in js]))
        _softmax_steps(streams, m_ref, acc_ref)

    _chunk_pairs(i, step)
    step([i], masked=True)
    for g in range(N_HEADS // 2):
        _store_pair(o_ref, g, _normalized(acc_ref, 2 * g), _normalized(acc_ref, 2 * g + 1))


def _fox(qa, ka, vat, *, batch, seq):
    n = batch * seq
    t = TILE
    nq = seq // t
    return pl.pallas_call(
        _fox_kernel,
        grid=(batch, nq),
        in_specs=[pl.BlockSpec((t, 512), lambda b, i: (b * nq + i, 0)),
                  pl.BlockSpec((seq, 512), lambda b, i: (b, 0)),
                  pl.BlockSpec((nq, N_HEADS * VT_ROWS, t), lambda b, i: (b, 0, 0))],
        out_specs=pl.BlockSpec((t, 256), lambda b, i: (b * nq + i, 0)),
        out_shape=jax.ShapeDtypeStruct((n, 256), BF16),
        scratch_shapes=[pltpu.VMEM((N_HEADS, 1, t), F32), pltpu.VMEM((N_HEADS, VT_ROWS, t), F32)],
        compiler_params=_cparams(("arbitrary", "arbitrary")),
        name="fox",
    )(qa, ka, vat)


def _diff_kernel(q_ref, k_ref, vt_ref, dl_ref, li_ref, g_ref, o_ref, qm_ref, m_ref, acc_ref):
    i = pl.program_id(1)
    t = q_ref.shape[0]
    _softmax_init(m_ref, acc_ref)
    _masked_heads(q_ref, qm_ref, DIFF_QK_DIM)

    def step(js, masked=False):
        for g in range(N_HEADS // 2):
            ks = [k_ref[pl.ds(pl.multiple_of(j * t, t), t), g * LANES:(g + 1) * LANES] for j in js]
            streams = []
            for idx in range(4 * g, 4 * g + 4):
                sts = [_dot_t(k, qm_ref[idx]) for k in ks]
                if masked:
                    sts = [jnp.where(_diag_mask(t), st, NEG) for st in sts]
                h = idx // 2
                streams.append((idx, sts, [vt_ref[j, h * VT_ROWS:(h + 1) * VT_ROWS, :] for j in js]))
            _softmax_steps(streams, m_ref, acc_ref)

    _chunk_pairs(i, step)
    step([i], masked=True)

    dl = dl_ref[...]
    lam_init = li_ref[...]
    lam = (jnp.exp(jnp.sum(dl[0:1] * dl[1:2], axis=-1, keepdims=True))
           - jnp.exp(jnp.sum(dl[2:3] * dl[3:4], axis=-1, keepdims=True)) + lam_init)
    scale_row = g_ref[...] * (1.0 - lam_init)
    for g in range(N_HEADS // 2):
        outs = []
        for h in (2 * g, 2 * g + 1):
            o = _normalized(acc_ref, 2 * h) - lam * _normalized(acc_ref, 2 * h + 1)
            outs.append(o * lax.rsqrt(jnp.mean(o * o, axis=0, keepdims=True) + NORM_EPS))
        _store_pair(o_ref, g, outs[0], outs[1], scale_row)


def _diff(qc, kc, vct, dl, lam_init, g2, *, batch, seq):
    n = batch * seq
    t = TILE
    nq = seq // t
    fixed = lambda b, i: (0, 0)
    return pl.pallas_call(
        _diff_kernel,
        grid=(batch, nq),
        in_specs=[pl.BlockSpec((t, 256), lambda b, i: (b * nq + i, 0)),
                  pl.BlockSpec((seq, 256), lambda b, i: (b, 0)),
                  pl.BlockSpec((nq, N_HEADS * VT_ROWS, t), lambda b, i: (b, 0, 0)),
                  pl.BlockSpec((4, DIFF_QK_DIM), fixed), pl.BlockSpec((1, 1), fixed),
                  pl.BlockSpec((1, LANES), fixed)],
        out_specs=pl.BlockSpec((t, 256), lambda b, i: (b * nq + i, 0)),
        out_shape=jax.ShapeDtypeStruct((n, 256), BF16),
        scratch_shapes=[pltpu.VMEM((2 * N_HEADS, t, LANES), BF16),
                        pltpu.VMEM((2 * N_HEADS, 1, t), F32),
                        pltpu.VMEM((2 * N_HEADS, VT_ROWS, t), F32)],
        compiler_params=_cparams(("arbitrary", "arbitrary")),
        name="diff",
    )(qc, kc, vct, dl, lam_init, g2)


def _mem_kernel(q_ref, mem_ref, g_ref, wk_ref, wvt_ref, o_ref, k_s, vt_s, qm_ref):
    i = pl.program_id(1)
    m_len = mem_ref.shape[1]

    @pl.when(i == 0)
    def _():
        hm = _rms(mem_ref[0], g_ref[...]).astype(BF16)
        k_s[...] = _dot(hm, wk_ref[...]).astype(BF16)
        vt = _dot_t(wvt_ref[...], hm)
        ones = _ones_rows(m_len)
        for h in range(N_HEADS):
            vt_s[h, 0:HEAD_DIM, :] = vt[h * HEAD_DIM:(h + 1) * HEAD_DIM, :].astype(BF16)
            vt_s[h, HEAD_DIM:VT_ROWS, :] = ones

    _masked_heads(q_ref, qm_ref, HEAD_DIM)
    outs = []
    for h in range(N_HEADS):
        g = h // 2
        st = _dot_t(k_s[:, g * LANES:(g + 1) * LANES], qm_ref[h])
        p = jnp.exp(st - jnp.max(st, axis=0, keepdims=True)).astype(BF16)
        acc = _dot(vt_s[h], p)
        outs.append(acc[0:HEAD_DIM, :] / acc[HEAD_DIM:HEAD_DIM + 1, :])
    for g in range(N_HEADS // 2):
        _store_pair(o_ref, g, outs[2 * g], outs[2 * g + 1])


def _mem_attn(qm, mem, g, wk, wvt, *, batch, seq):
    n = batch * seq
    t = TILE
    nq = seq // t
    m_len, d = mem.shape[1], mem.shape[2]
    fixed = lambda b, i: (0, 0)
    return pl.pallas_call(
        _mem_kernel,
        grid=(batch, nq),
        in_specs=[pl.BlockSpec((t, 256), lambda b, i: (b * nq + i, 0)),
                  pl.BlockSpec((1, m_len, d), lambda b, i: (b, 0, 0)),
                  pl.BlockSpec((1, d), fixed), pl.BlockSpec((d, 256), fixed), pl.BlockSpec((256, d), fixed)],
        out_specs=pl.BlockSpec((t, 256), lambda b, i: (b * nq + i, 0)),
        out_shape=jax.ShapeDtypeStruct((n, 256), BF16),
        scratch_shapes=[pltpu.VMEM((m_len, 256), BF16), pltpu.VMEM((N_HEADS, VT_ROWS, m_len), BF16),
                        pltpu.VMEM((N_HEADS, t, LANES), BF16)],
        compiler_params=_cparams(("arbitrary", "arbitrary")),
        name="mem_attn",
    )(qm, mem, g, wk, wvt)


def _dsa_kernel(qi_ref, wit_ref, qb_ref, ki_ref, kb_ref, vbt_ref, o_ref,
                keys_ref, qim_ref, qbm_ref, m_ref, acc_ref, *, top_k, idx_bits):
    i = pl.program_id(1)
    t = qi_ref.shape[0]
    n_ch = i + 1

    lane_q = lax.broadcasted_iota(I32, (t, 2 * LANES), 1)
    qi = qi_ref[...]
    for h in range(IDX_HEADS):
        qim_ref[h] = jnp.where((lane_q >> 5) == h, qi, jnp.zeros_like(qi))
    wit = wit_ref[...]

    def score_step(js, masked=False):
        kcs = [ki_ref[pl.ds(pl.multiple_of(j * t, t), t), :] for j in js]
        scores = [jnp.zeros((t, t), F32) for _ in js]
        for h in range(IDX_HEADS):
            w_h = wit[h:h + 1, :]
            scores = [sc + jnp.maximum(_dot_t(kc, qim_ref[h]), 0.0) * w_h for sc, kc in zip(scores, kcs)]
        for j, score in zip(js, scores):
            bits = lax.bitcast_convert_type(score, I32)
            key = bits ^ ((bits >> 31) & jnp.int32(0x7FFFFFFF))
            if masked:
                key = jnp.where(_diag_mask(t), key, jnp.int32(INT_MIN))
            keys_ref[j] = key

    _chunk_pairs(i, score_step)
    score_step([i], masked=True)
    keys_ref[n_ch] = jnp.full((t, t), INT_MIN, I32)
    n_pairs = (n_ch + 1) // 2

    def count(hits):
        def body(jj, acc):
            for j in (2 * jj, 2 * jj + 1):
                acc = acc + jnp.sum(hits(keys_ref[j], j).reshape(t // 8, 8, t), axis=0)
            return acc
        acc = lax.fori_loop(0, n_pairs, body, jnp.zeros((8, t), F32))
        return jnp.sum(acc, axis=0, keepdims=True)

    n_valid = i * t + lax.broadcasted_iota(I32, (1, t), 1) + 1

    def unsettled(cnt):
        return jnp.max(jnp.where(n_valid > top_k, jnp.where(cnt != top_k, 1.0, 0.0), 0.0)) > 0.0

    def bit_step(carry):
        b, thr, cnt_thr, _ = carry
        for s in range(BITS_PER_CHECK):
            cand = thr + (jnp.int32(1) << (31 - b - s))
            cnt = count(lambda kc, j, cand=cand: jnp.where(kc >= cand, 1.0, 0.0))
            take = cnt >= top_k
            cnt_thr = jnp.where(take, cnt, cnt_thr)
            thr = jnp.where(take, cand, thr)
        return b + BITS_PER_CHECK, thr, cnt_thr, unsettled(cnt_thr)

    cnt0 = jnp.full((1, t), 2.0 * t, F32) * n_pairs.astype(F32)
    _, thr, n_ge, _ = lax.while_loop(
        lambda c: jnp.logical_and(c[0] < 32, c[3]), bit_step,
        (jnp.int32(0), jnp.full((1, t), INT_MIN, I32), cnt0, unsettled(cnt0)))

    tie = jnp.where(thr > INT_MIN, jnp.where(n_ge > top_k, 1.0, 0.0), 0.0)

    @pl.when(jnp.max(tie) > 0.0)
    def _():
        n_gt = count(lambda kc, j: jnp.where(kc > thr, 1.0, 0.0))
        need = top_k - n_gt

        def key_pos(j):
            return j * t + lax.broadcasted_iota(I32, (t, t), 0)

        def idx_step(b, cut):
            cand = cut + (jnp.int32(1) << (idx_bits - 1 - b))
            cnt = count(lambda kc, j: jnp.where(kc == thr, jnp.where(key_pos(j) < cand, 1.0, 0.0), 0.0))
            return jnp.where(cnt < need, cand, cut)

        cut = lax.fori_loop(0, idx_bits, idx_step, jnp.zeros((1, t), I32))
        cut = jnp.where(tie > 0.0, cut, jnp.int32(2 ** 30))

        def demote(j, c):
            kc = keys_ref[j]
            demoted = jnp.where(key_pos(j) > cut, kc - 1, kc)
            keys_ref[j] = jnp.where(kc == thr, demoted, kc)
            return c

        lax.fori_loop(0, n_ch, demote, 0)

    thr_sel = jnp.maximum(thr, jnp.int32(INT_MIN + 1))

    _softmax_init(m_ref, acc_ref)
    _masked_heads(qb_ref, qbm_ref, HEAD_DIM)

    def attn_step(js):
        sels = [keys_ref[j] >= thr_sel for j in js]
        ks = [kb_ref[pl.ds(pl.multiple_of(j * t, t), t), :] for j in js]
        vts = [vbt_ref[j] for j in js]
        streams = []
        for h in range(N_HEADS):
            sts = [jnp.where(sel, _dot_t(k, qbm_ref[h]), NEG) for sel, k in zip(sels, ks)]
            streams.append((h, sts, vts))
        _softmax_steps(streams, m_ref, acc_ref)

    _chunk_pairs(n_ch, attn_step)
    for g in range(N_HEADS // 2):
        _store_pair(o_ref, g, _normalized(acc_ref, 2 * g), _normalized(acc_ref, 2 * g + 1))


def _dsa(qi, wit, qb, ki, kb, vbt, *, batch, seq):
    n = batch * seq
    t = TILE
    nq = seq // t
    top_k = min(IDX_TOPK_MAX, seq // 4)
    idx_bits = max(1, int(math.ceil(math.log2(seq))))
    qrow = lambda b, i: (b * nq + i, 0)
    kv = lambda b, i: (b, 0)
    return pl.pallas_call(
        functools.partial(_dsa_kernel, top_k=top_k, idx_bits=idx_bits),
        grid=(batch, nq),
        in_specs=[pl.BlockSpec((t, 256), qrow), pl.BlockSpec((BF16_ROWS, t), lambda b, i: (0, b * nq + i)),
                  pl.BlockSpec((t, 256), qrow),
                  pl.BlockSpec((seq, 256), kv), pl.BlockSpec((seq, LANES), kv),
                  pl.BlockSpec((nq, VT_ROWS, t), lambda b, i: (b, 0, 0))],
        out_specs=pl.BlockSpec((t, 256), qrow),
        out_shape=jax.ShapeDtypeStruct((n, 256), BF16),
        scratch_shapes=[pltpu.VMEM((nq + 1, t, t), I32),
                        pltpu.VMEM((IDX_HEADS, t, 2 * LANES), BF16),
                        pltpu.VMEM((N_HEADS, t, LANES), BF16),
                        pltpu.VMEM((N_HEADS, 1, t), F32), pltpu.VMEM((N_HEADS, VT_ROWS, t), F32)],
        compiler_params=_cparams(("arbitrary", "arbitrary")),
        name="dsa",
    )(qi, wit, qb, ki, kb, vbt)


def _merge_kernel(x_ref, g_ref, oa_ref, ob_ref, oc_ref, om_ref, wg_ref, bg_ref, wb_ref, wo_ref, y_ref):
    x = x_ref[...]
    d = x.shape[1]
    h = _rms(x, g_ref[...]).astype(BF16)
    acc = jnp.zeros(x.shape, F32)
    for nb, o_ref in enumerate((oa_ref, ob_ref, oc_ref, om_ref)):
        gate = jax.nn.sigmoid(_dot(h, wg_ref[:, nb * d:(nb + 1) * d]) + bg_ref[:, nb * d:(nb + 1) * d])
        acc = acc + gate * _dot(o_ref[...], wb_ref[nb])
    y_ref[...] = x + _dot(acc.astype(BF16), wo_ref[...])


def _merge(x2, g, oa, ob, oc, om, wg, bg, wb, wo):
    n, d = x2.shape
    tm = TILE
    row = lambda i: (i, 0)
    fixed2 = lambda i: (0, 0)
    bw = oa.shape[1]
    return pl.pallas_call(
        _merge_kernel,
        grid=(n // tm,),
        in_specs=[pl.BlockSpec((tm, d), row), pl.BlockSpec((1, d), fixed2),
                  pl.BlockSpec((tm, bw), row), pl.BlockSpec((tm, bw), row),
                  pl.BlockSpec((tm, bw), row), pl.BlockSpec((tm, bw), row),
                  pl.BlockSpec((d, N_BRANCH * d), fixed2), pl.BlockSpec((1, N_BRANCH * d), fixed2),
                  pl.BlockSpec((N_BRANCH, bw, d), lambda i: (0, 0, 0)), pl.BlockSpec((d, d), fixed2)],
        out_specs=pl.BlockSpec((tm, d), row),
        out_shape=jax.ShapeDtypeStruct((n, d), F32),
        compiler_params=_cparams(("arbitrary",)),
        name="merge",
    )(x2, g, oa, ob, oc, om, wg, bg, wb, wo)


def _ffn_kernel(x_ref, g_ref, wi_ref, wo_ref, gf_ref, y_ref, *, hidden, chunk, final):
    x = x_ref[...]
    h = _rms(x, g_ref[...]).astype(BF16)
    acc = jnp.zeros(x.shape, F32)
    for c in range(hidden // chunk):
        a = _dot(h, wi_ref[:, c * chunk:(c + 1) * chunk])
        b = _dot(h, wi_ref[:, hidden + c * chunk:hidden + (c + 1) * chunk])
        t = (a * jax.nn.sigmoid(a)) * b
        acc = acc + _dot(t.astype(BF16), wo_ref[c * chunk:(c + 1) * chunk, :])
    y = x + acc
    if final:
        y = _rms(y, gf_ref[...])
    y_ref[...] = y


def _ffn(x2, g, w_in, w_out, g_final, *, final):
    n, d = x2.shape
    tm = TILE
    hidden = w_out.shape[0]
    chunk = hidden // 4 if (hidden // 4) % LANES == 0 else hidden
    row = lambda i: (i, 0)
    fixed2 = lambda i: (0, 0)
    return pl.pallas_call(
        functools.partial(_ffn_kernel, hidden=hidden, chunk=chunk, final=final),
        grid=(n // tm,),
        in_specs=[pl.BlockSpec((tm, d), row), pl.BlockSpec((1, d), fixed2),
                  pl.BlockSpec((d, 2 * hidden), fixed2), pl.BlockSpec((hidden, d), fixed2),
                  pl.BlockSpec((1, d), fixed2)],
        out_specs=pl.BlockSpec((tm, d), row),
        out_shape=jax.ShapeDtypeStruct((n, d), F32),
        compiler_params=_cparams(("arbitrary",)),
        name="ffn",
    )(x2, g, w_in, w_out, g_final)


_IN_SPLITS = (256, 256, 256, 4, 256, 64, 64, 256, 32, 8, 256, 256, 256, 256)


def _pack_w_in(w):
    d = w.shape[0]
    offs = np.cumsum((0,) + _IN_SPLITS)
    qa, ka, va, fa, qb, kb, vb, qi, ki, wi, qc, kc, vc, qm = (
        w[:, offs[k]:offs[k + 1]] for k in range(len(_IN_SPLITS)))

    def pad_heads(a):
        a = a.reshape(d, N_HEADS, HEAD_DIM)
        return jnp.pad(a, ((0, 0), (0, 0), (0, LANES - HEAD_DIM))).reshape(d, N_HEADS * LANES)

    def pad_to(a, wd):
        return jnp.pad(a, ((0, 0), (0, wd - a.shape[1])))

    cols = [pad_heads(qa), pad_heads(ka), pad_to(fa, LANES), qb, jnp.concatenate([kb, kb], axis=1),
            qi, jnp.tile(ki, (1, IDX_HEADS)), qc, kc, qm]
    rows = [va, vc, vb, pad_to(wi, BF16_ROWS)]
    return (jnp.concatenate(cols, axis=1).astype(BF16), jnp.concatenate(rows, axis=1).T.astype(BF16))


def _fox_consts():
    eq = np.zeros((3, LANES, 512), np.float32)
    ek = np.zeros((3, LANES, 512), np.float32)
    oq = np.zeros((1, 512), np.float32)
    ok = np.zeros((1, 512), np.float32)
    for h in range(N_HEADS):
        base = h * LANES + HEAD_DIM
        for t in range(3):
            eq[t, h, base + t] = 1.0
            ek[t, h, base + 3 + t] = -1.0
            oq[0, base + 3 + t] = 1.0
            ok[0, base + t] = 1.0
    return (jnp.asarray(eq, BF16), jnp.asarray(ek, BF16), jnp.asarray(oq), jnp.asarray(ok))


def _rope_tables(positions, rot_dim, period):
    half = rot_dim // 2
    inv_freq = ROPE_THETA ** (-jnp.arange(0, rot_dim, 2, dtype=F32) / rot_dim)
    ang = positions.astype(F32).reshape(-1, 1) * inv_freq
    cos, sin = jnp.cos(ang), jnp.sin(ang)
    n = ang.shape[0]
    c = jnp.concatenate([cos, cos, jnp.ones((n, period - 2 * half), F32)], axis=1)
    s = jnp.concatenate([-sin, sin, jnp.zeros((n, period - 2 * half), F32)], axis=1)
    reps = LANES // period
    return jnp.tile(c, (1, reps)), jnp.tile(s, (1, reps))


def kernel(x, mem, positions, norm_mix_g, w_in, b_forget, diff_lambda, diff_subln_g, norm_mem_g,
           w_mem_kv, w_branch, w_gate, b_gate, w_out, norm_ffn_g, w_ffn_in, w_ffn_out, final_norm_g):
    batch, seq, d = x.shape
    depth = w_in.shape[0]
    n = batch * seq
    assert seq % TILE == 0, seq

    c16, s16 = _rope_tables(positions, HEAD_DIM // 4, HEAD_DIM)
    c8, s8 = _rope_tables(positions, DIFF_QK_DIM // 4, DIFF_QK_DIM)
    consts = _fox_consts()

    x2 = x.reshape(n, d)
    for l in range(depth):
        lam_init = 0.8 - 0.6 * math.exp(-0.3 * l)
        bf = jnp.pad(b_forget[l], (0, LANES - b_forget.shape[1])).reshape(1, LANES)
        w_row, w_t = _pack_w_in(w_in[l])
        z = _proj(x2, norm_mix_g[l].reshape(1, d), w_row, w_t, bf, (c16, s16, c8, s8), consts, seq=seq)
        o_a = _fox(z["qa"], z["ka"], z["vaT"], batch=batch, seq=seq)
        o_b = _dsa(z["qi"], z["wiT"], z["qb"], z["ki"], z["kb"], z["vbT"], batch=batch, seq=seq)
        o_c = _diff(z["qc"], z["kc"], z["vcT"], diff_lambda[l], jnp.full((1, 1), lam_init, F32),
                    jnp.tile(diff_subln_g[l], 2).reshape(1, LANES), batch=batch, seq=seq)
        w_kv = w_mem_kv[l]
        o_m = _mem_attn(z["qm"], mem, norm_mem_g[l].reshape(1, d), w_kv[:, :256].astype(BF16),
                        w_kv[:, 256:].T.astype(BF16), batch=batch, seq=seq)
        x2 = _merge(x2, norm_mix_g[l].reshape(1, d), o_a, o_b, o_c, o_m,
                    w_gate[l].astype(BF16), b_gate[l].reshape(1, -1), w_branch[l].astype(BF16),
                    w_out[l].astype(BF16))
        x2 = _ffn(x2, norm_ffn_g[l].reshape(1, d), w_ffn_in[l].astype(BF16), w_ffn_out[l].astype(BF16),
                  final_norm_g.reshape(1, d), final=(l == depth - 1))
    return x2.reshape(batch, seq, d)
```

```python
import functools
import math

import numpy as np
import jax
import jax.numpy as jnp
from jax import lax
from jax.experimental import pallas as pl
from jax.experimental.pallas import tpu as pltpu

F32 = jnp.float32
BF16 = jnp.bfloat16
I32 = jnp.int32

HEAD_DIM = 64
N_BRANCH = 4
N_HEADS = 4
IDX_HEADS = 8
IDX_DIM = 32
IDX_TOPK_MAX = 256
DIFF_QK_DIM = 32
ROPE_THETA = 500000.0
NORM_EPS = 1e-6
LANES = 128
BF16_ROWS = 16
NEG = -1e30
LOG2E = math.log2(math.e)
INT_MIN = -2 ** 31
TILE = 256
VT_ROWS = HEAD_DIM + BF16_ROWS
BITS_PER_CHECK = 4
KEY_PAD = 1

VMEM_LIMIT = 56 * 1024 * 1024

_SEG = {}
_off = 0
for _name, _w in (("qa", 512), ("ka", 512), ("fa", 128), ("qb", 256), ("kb", 128),
                  ("qi", 256), ("ki", 256), ("qc", 256), ("kc", 256), ("qm", 256)):
    _SEG[_name] = (_off, _off + _w)
    _off += _w
PROJ_COLS = _off
_TSEG = {"va": (0, 256), "vc": (256, 512), "vb": (512, 576), "wi": (576, 592)}
PROJ_T_ROWS = 592

_DN_T = (((1,), (1,)), ((), ()))


def _dot(a, b):
    return jnp.dot(a, b, preferred_element_type=F32)


def _dot_t(a, b):
    return lax.dot_general(a, b, _DN_T, preferred_element_type=F32)


def _rms(x, g):
    return x * lax.rsqrt(jnp.mean(x * x, axis=-1, keepdims=True) + NORM_EPS) * g


def _cparams(sem):
    return pltpu.CompilerParams(dimension_semantics=sem, vmem_limit_bytes=VMEM_LIMIT)


def _split3(x):
    hi = x.astype(BF16)
    r = x - hi.astype(F32)
    mid = r.astype(BF16)
    lo = (r - mid.astype(F32)).astype(BF16)
    return hi, mid, lo


def _ones_rows(cols):
    r = lax.broadcasted_iota(I32, (BF16_ROWS, cols), 0)
    return jnp.where(r == 0, 1.0, 0.0).astype(BF16)


def _rope(z, c, s, half, period):
    lane = lax.broadcasted_iota(I32, z.shape, 1)
    up = pltpu.roll(z, LANES - half, 1)
    dn = pltpu.roll(z, half, 1)
    partner = jnp.where((lane & (period - 1)) < half, up, dn)
    return z * c + partner * s


def _proj_kernel(x_ref, g_ref, w_ref, wt_ref, bf_ref, c16_ref, s16_ref, c8_ref, s8_ref, eq_ref, ek_ref,
                 oq_ref, ok_ref,
                 qa_o, ka_o, qb_o, kb_o, qi_o, ki_o, qc_o, kc_o, qm_o, vat_o, vct_o, vbt_o, wit_o,
                 carry_ref, *, tiles_per_batch):
    i = pl.program_id(0)
    tm = x_ref.shape[0]
    h = _rms(x_ref[...], g_ref[...]).astype(BF16)

    def mm(name):
        lo, hi = _SEG[name]
        return _dot(h, w_ref[:, lo:hi])

    f = mm("fa") + bf_ref[...]
    ls = jnp.minimum(f, 0.0) - jnp.log(1.0 + jnp.exp(-jnp.abs(f)))
    r_i = lax.broadcasted_iota(I32, (tm, tm), 0)
    c_i = lax.broadcasted_iota(I32, (tm, tm), 1)
    ltri = jnp.where(c_i <= r_i, 1.0, 0.0).astype(BF16)
    p_hi, p_mid, p_lo = _split3(ls)
    pre = _dot(ltri, p_hi) + _dot(ltri, p_mid) + _dot(ltri, p_lo)

    @pl.when(i % tiles_per_batch == 0)
    def _():
        carry_ref[...] = jnp.zeros_like(carry_ref)

    cum = pre + carry_ref[...]
    carry_ref[...] = cum[tm - 1:tm, :]
    c_hi, c_mid, c_lo = _split3(cum * LOG2E)

    qa = mm("qa") * (HEAD_DIM ** -0.5 * LOG2E)
    qa = qa + _dot(c_hi, eq_ref[0]) + _dot(c_mid, eq_ref[1]) + _dot(c_lo, eq_ref[2]) + oq_ref[...]
    qa_o[...] = qa.astype(BF16)
    ka = mm("ka")
    ka = ka + _dot(c_hi, ek_ref[0]) + _dot(c_mid, ek_ref[1]) + _dot(c_lo, ek_ref[2]) + ok_ref[...]
    ka_o[...] = ka.astype(BF16)

    c16 = c16_ref[...]
    s16 = s16_ref[...]
    c8 = c8_ref[...]
    s8 = s8_ref[...]

    def roped(name, out_ref, c, s, half, period, scale):
        z = mm(name)
        for k in range(z.shape[1] // LANES):
            r = _rope(z[:, k * LANES:(k + 1) * LANES], c, s, half, period)
            if scale != 1.0:
                r = r * scale
            out_ref[:, k * LANES:(k + 1) * LANES] = r.astype(BF16)

    roped("qb", qb_o, c16, s16, 8, 64, HEAD_DIM ** -0.5 * LOG2E)
    roped("kb", kb_o, c16, s16, 8, 64, 1.0)
    roped("qi", qi_o, c8, s8, 4, 32, 1.0)
    roped("ki", ki_o, c8, s8, 4, 32, 1.0)
    roped("qc", qc_o, c8, s8, 4, 32, DIFF_QK_DIM ** -0.5 * LOG2E)
    roped("kc", kc_o, c8, s8, 4, 32, 1.0)
    qm_o[...] = (mm("qm") * (HEAD_DIM ** -0.5 * LOG2E)).astype(BF16)

    zt = _dot_t(wt_ref[...], h)
    ones = _ones_rows(tm)
    for name, out_ref, heads in (("va", vat_o, N_HEADS), ("vc", vct_o, N_HEADS), ("vb", vbt_o, 1)):
        lo = _TSEG[name][0]
        for hd in range(heads):
            out_ref[0, hd * VT_ROWS:hd * VT_ROWS + HEAD_DIM, :] = (
                zt[lo + hd * HEAD_DIM:lo + (hd + 1) * HEAD_DIM, :].astype(BF16))
            out_ref[0, hd * VT_ROWS + HEAD_DIM:(hd + 1) * VT_ROWS, :] = ones
    lo, hi = _TSEG["wi"]
    wit_o[...] = zt[lo:hi, :] * (1.0 / 16.0)


def _proj(x2, g, w, wt, bf, tabs, consts, *, seq):
    n, d = x2.shape
    tm = TILE
    grid = (n // tm,)
    row = lambda i: (i, 0)
    fixed2 = lambda i: (0, 0)
    fixed3 = lambda i: (0, 0, 0)
    chunk3 = lambda i: (i, 0, 0)
    eq, ek, oq, ok = consts
    names = ("qa", "ka", "qb", "kb", "qi", "ki", "qc", "kc", "qm")
    out_specs = [pl.BlockSpec((tm, _SEG[nm][1] - _SEG[nm][0]), row) for nm in names]
    out_shape = [jax.ShapeDtypeStruct((n, _SEG[nm][1] - _SEG[nm][0]), BF16) for nm in names]
    for heads in (N_HEADS, N_HEADS, 1):
        out_specs.append(pl.BlockSpec((1, heads * VT_ROWS, tm), chunk3))
        out_shape.append(jax.ShapeDtypeStruct((n // tm, heads * VT_ROWS, tm), BF16))
    out_specs.append(pl.BlockSpec((BF16_ROWS, tm), lambda i: (0, i)))
    out_shape.append(jax.ShapeDtypeStruct((BF16_ROWS, n), F32))
    outs = pl.pallas_call(
        functools.partial(_proj_kernel, tiles_per_batch=seq // tm),
        grid=grid,
        in_specs=[pl.BlockSpec((tm, d), row), pl.BlockSpec((1, d), fixed2),
                  pl.BlockSpec((d, PROJ_COLS), fixed2), pl.BlockSpec((PROJ_T_ROWS, d), fixed2),
                  pl.BlockSpec((1, LANES), fixed2),
                  pl.BlockSpec((tm, LANES), row), pl.BlockSpec((tm, LANES), row),
                  pl.BlockSpec((tm, LANES), row), pl.BlockSpec((tm, LANES), row),
                  pl.BlockSpec((3, LANES, 512), fixed3), pl.BlockSpec((3, LANES, 512), fixed3),
                  pl.BlockSpec((1, 512), fixed2), pl.BlockSpec((1, 512), fixed2)],
        out_specs=out_specs,
        out_shape=out_shape,
        scratch_shapes=[pltpu.VMEM((1, LANES), F32)],
        compiler_params=_cparams(("arbitrary",)),
        name="proj",
    )(x2, g, w, wt, bf, *tabs, eq, ek, oq, ok)
    return dict(zip(names + ("vaT", "vcT", "vbT", "wiT"), outs))


def _softmax_init(m_ref, acc_ref):
    m_ref[...] = jnp.full(m_ref.shape, NEG, F32)
    acc_ref[...] = jnp.zeros(acc_ref.shape, F32)


def _softmax_steps(streams, m_ref, acc_ref):
    n = len(streams)
    sts, alphas, ps = [None] * n, [None] * n, [None] * n

    def scores(k):
        sts[k] = streams[k][1]()

    def probs(k):
        idx = streams[k][0]
        m_prev = m_ref[idx]
        m_new = m_prev
        for st in sts[k]:
            m_new = jnp.maximum(m_new, jnp.max(st, axis=0, keepdims=True))
        alphas[k] = jnp.exp2(m_prev - m_new)
        ps[k] = [jnp.exp2(st - m_new).astype(BF16) for st in sts[k]]
        m_ref[idx] = m_new

    def values(k):
        idx, _, vts = streams[k]
        pv = _dot(vts[0], ps[k][0])
        for vt, p in zip(vts[1:], ps[k][1:]):
            pv = pv + _dot(vt, p)
        acc_ref[idx] = alphas[k] * acc_ref[idx] + pv

    for stage in (scores, probs, values):
        for k in range(n):
            stage(k)


def _chunk_pairs(n_full, step):
    def body(jj, c):
        step([2 * jj, 2 * jj + 1])
        return c

    lax.fori_loop(0, n_full // 2, body, 0)

    @pl.when(n_full % 2 == 1)
    def _():
        step([n_full - 1])


def _pipelined_steps(n_steps, score_tiles, value_tiles, idxs, bufs, m_ref, acc_ref):
    def produce(n, st_buf, mx_buf):
        tiles = score_tiles(n)
        for k, sts in enumerate(tiles):
            mx = None
            for c, st in enumerate(sts):
                st_buf[k * len(sts) + c] = st
                cm = jnp.max(st, axis=0, keepdims=True)
                mx = cm if mx is None else jnp.maximum(mx, cm)
            mx_buf[k] = mx

    def consume(n, st_buf, mx_buf):
        vts = value_tiles(n)
        n_c = len(vts[0])
        alphas, ps = [], []
        for k, idx in enumerate(idxs):
            m_prev = m_ref[idx]
            m_new = jnp.maximum(m_prev, mx_buf[k])
            alphas.append(jnp.exp2(m_prev - m_new))
            ps.append([jnp.exp2(st_buf[k * n_c + c] - m_new).astype(BF16) for c in range(n_c)])
            m_ref[idx] = m_new
        for k, idx in enumerate(idxs):
            pv = _dot(vts[k][0], ps[k][0])
            for c in range(1, n_c):
                pv = pv + _dot(vts[k][c], ps[k][c])
            acc_ref[idx] = alphas[k] * acc_ref[idx] + pv

    (st_a, mx_a), (st_b, mx_b) = bufs

    @pl.when(n_steps > 0)
    def _():
        produce(0, st_a, mx_a)

    def body(s, c):
        produce(2 * s + 1, st_b, mx_b)
        consume(2 * s, st_a, mx_a)
        produce(2 * s + 2, st_a, mx_a)
        consume(2 * s + 1, st_b, mx_b)
        return c

    lax.fori_loop(0, n_steps // 2, body, 0)

    @pl.when(n_steps % 2 == 1)
    def _():
        consume(n_steps - 1, st_a, mx_a)


def _normalized(acc_ref, idx):
    acc = acc_ref[idx]
    return acc[0:HEAD_DIM, :] / acc[HEAD_DIM:HEAD_DIM + 1, :]


def _store_pair(o_ref, g, o0, o1, scale_row=None):
    o = jnp.concatenate([o0, o1], axis=0).T
    if scale_row is not None:
        o = o * scale_row
    o_ref[:, g * LANES:(g + 1) * LANES] = o.astype(o_ref.dtype)


def _diag_mask(t):
    key = lax.broadcasted_iota(I32, (t, t), 0)
    qry = lax.broadcasted_iota(I32, (t, t), 1)
    return key <= qry


def _masked_heads(q_ref, qm_ref, width):
    tq = q_ref.shape[0]
    lane = lax.broadcasted_iota(I32, (tq, LANES), 1)
    per_group = LANES // width
    for idx in range(2 * per_group):
        g = idx // per_group
        qg = q_ref[:, g * LANES:(g + 1) * LANES]
        lo = (idx % per_group) * width
        sel = (lane >= lo) & (lane < lo + width)
        qm_ref[idx] = jnp.where(sel, qg, jnp.zeros_like(qg))


def _fox_kernel(q_ref, k_ref, vt_ref, o_ref, m_ref, acc_ref, sta_ref, mxa_ref, stb_ref, mxb_ref):
    i = pl.program_id(1)
    t = q_ref.shape[0]
    nq = vt_ref.shape[0]
    _softmax_init(m_ref, acc_ref)

    def chunks_of(n):
        return [jnp.minimum(j, nq - 1) for j in (2 * n, 2 * n + 1)]

    def score_tiles(n):
        return [[_dot_t(k_ref[pl.ds(pl.multiple_of(j * t, t), t), h * LANES:(h + 1) * LANES],
                        q_ref[:, h * LANES:(h + 1) * LANES]) for j in chunks_of(n)] for h in range(N_HEADS)]

    def value_tiles(n):
        return [[vt_ref[j, h * VT_ROWS:(h + 1) * VT_ROWS, :] for j in chunks_of(n)] for h in range(N_HEADS)]

    _pipelined_steps(i // 2, score_tiles, value_tiles, list(range(N_HEADS)),
                     ((sta_ref, mxa_ref), (stb_ref, mxb_ref)), m_ref, acc_ref)

    def step(js, masked=False):
        streams = []
        for h in range(N_HEADS):
            def scores(h=h):
                q = q_ref[:, h * LANES:(h + 1) * LANES]
                sts = [_dot_t(k_ref[pl.ds(pl.multiple_of(j * t, t), t), h * LANES:(h + 1) * LANES], q)
                       for j in js]
                if masked:
                    sts = [jnp.where(_diag_mask(t), st, NEG) for st in sts]
                return sts
            streams.append((h, scores, [vt_ref[j, h * VT_ROWS:(h + 1) * VT_ROWS, :] for j in js]))
        _softmax_steps(streams, m_ref, acc_ref)

    @pl.when(i % 2 == 1)
    def _():
        step([i - 1])

    step([i], masked=True)
    for g in range(N_HEADS // 2):
        _store_pair(o_ref, g, _normalized(acc_ref, 2 * g), _normalized(acc_ref, 2 * g + 1))


def _fox(qa, ka, vat, *, batch, seq):
    n = batch * seq
    t = TILE
    nq = seq // t
    return pl.pallas_call(
        _fox_kernel,
        grid=(batch, nq),
        in_specs=[pl.BlockSpec((t, 512), lambda b, i: (b * nq + i, 0)),
                  pl.BlockSpec((seq, 512), lambda b, i: (b, 0)),
                  pl.BlockSpec((nq, N_HEADS * VT_ROWS, t), lambda b, i: (b, 0, 0))],
        out_specs=pl.BlockSpec((t, 256), lambda b, i: (b * nq + i, 0)),
        out_shape=jax.ShapeDtypeStruct((n, 256), BF16),
        scratch_shapes=[pltpu.VMEM((N_HEADS, 1, t), F32), pltpu.VMEM((N_HEADS, VT_ROWS, t), F32),
                        pltpu.VMEM((2 * N_HEADS, t, t), F32), pltpu.VMEM((N_HEADS, 1, t), F32),
                        pltpu.VMEM((2 * N_HEADS, t, t), F32), pltpu.VMEM((N_HEADS, 1, t), F32)],
        compiler_params=_cparams(("arbitrary", "arbitrary")),
        name="fox",
    )(qa, ka, vat)


def _diff_kernel(q_ref, k_ref, vt_ref, dl_ref, li_ref, g_ref, o_ref, qm_ref, m_ref, acc_ref,
                 sta_ref, mxa_ref, stb_ref, mxb_ref):
    i = pl.program_id(1)
    t = q_ref.shape[0]
    nq = vt_ref.shape[0]
    n_streams = 2 * N_HEADS
    _softmax_init(m_ref, acc_ref)
    _masked_heads(q_ref, qm_ref, DIFF_QK_DIM)

    def chunks_of(n):
        return [jnp.minimum(j, nq - 1) for j in (2 * n, 2 * n + 1)]

    def score_tiles(n):
        tiles = []
        for g in range(N_HEADS // 2):
            ks = [k_ref[pl.ds(pl.multiple_of(j * t, t), t), g * LANES:(g + 1) * LANES] for j in chunks_of(n)]
            tiles += [[_dot_t(k, qm_ref[idx]) for k in ks] for idx in range(4 * g, 4 * g + 4)]
        return tiles

    def value_tiles(n):
        return [[vt_ref[j, (idx // 2) * VT_ROWS:(idx // 2 + 1) * VT_ROWS, :] for j in chunks_of(n)]
                for idx in range(n_streams)]

    _pipelined_steps(i // 2, score_tiles, value_tiles, list(range(n_streams)),
                     ((sta_ref, mxa_ref), (stb_ref, mxb_ref)), m_ref, acc_ref)

    def step(js, masked=False):
        for g in range(N_HEADS // 2):
            ks = [k_ref[pl.ds(pl.multiple_of(j * t, t), t), g * LANES:(g + 1) * LANES] for j in js]
            streams = []
            for idx in range(4 * g, 4 * g + 4):
                def scores(idx=idx, ks=ks):
                    sts = [_dot_t(k, qm_ref[idx]) for k in ks]
                    if masked:
                        sts = [jnp.where(_diag_mask(t), st, NEG) for st in sts]
                    return sts
                h = idx // 2
                streams.append((idx, scores, [vt_ref[j, h * VT_ROWS:(h + 1) * VT_ROWS, :] for j in js]))
            _softmax_steps(streams, m_ref, acc_ref)

    @pl.when(i % 2 == 1)
    def _():
        step([i - 1])

    step([i], masked=True)

    dl = dl_ref[...]
    lam_init = li_ref[...]
    lam = (jnp.exp(jnp.sum(dl[0:1] * dl[1:2], axis=-1, keepdims=True))
           - jnp.exp(jnp.sum(dl[2:3] * dl[3:4], axis=-1, keepdims=True)) + lam_init)
    scale_row = g_ref[...] * (1.0 - lam_init)
    for g in range(N_HEADS // 2):
        outs = []
        for h in (2 * g, 2 * g + 1):
            o = _normalized(acc_ref, 2 * h) - lam * _normalized(acc_ref, 2 * h + 1)
            outs.append(o * lax.rsqrt(jnp.mean(o * o, axis=0, keepdims=True) + NORM_EPS))
        _store_pair(o_ref, g, outs[0], outs[1], scale_row)


def _diff(qc, kc, vct, dl, lam_init, g2, *, batch, seq):
    n = batch * seq
    t = TILE
    nq = seq // t
    fixed = lambda b, i: (0, 0)
    return pl.pallas_call(
        _diff_kernel,
        grid=(batch, nq),
        in_specs=[pl.BlockSpec((t, 256), lambda b, i: (b * nq + i, 0)),
                  pl.BlockSpec((seq, 256), lambda b, i: (b, 0)),
                  pl.BlockSpec((nq, N_HEADS * VT_ROWS, t), lambda b, i: (b, 0, 0)),
                  pl.BlockSpec((4, DIFF_QK_DIM), fixed), pl.BlockSpec((1, 1), fixed),
                  pl.BlockSpec((1, LANES), fixed)],
        out_specs=pl.BlockSpec((t, 256), lambda b, i: (b * nq + i, 0)),
        out_shape=jax.ShapeDtypeStruct((n, 256), BF16),
        scratch_shapes=[pltpu.VMEM((2 * N_HEADS, t, LANES), BF16),
                        pltpu.VMEM((2 * N_HEADS, 1, t), F32),
                        pltpu.VMEM((2 * N_HEADS, VT_ROWS, t), F32),
                        pltpu.VMEM((4 * N_HEADS, t, t), F32), pltpu.VMEM((2 * N_HEADS, 1, t), F32),
                        pltpu.VMEM((4 * N_HEADS, t, t), F32), pltpu.VMEM((2 * N_HEADS, 1, t), F32)],
        compiler_params=_cparams(("arbitrary", "arbitrary")),
        name="diff",
    )(qc, kc, vct, dl, lam_init, g2)


def _mem_kernel(q_ref, mem_ref, g_ref, wk_ref, wvt_ref, o_ref, k_s, vt_s, qm_ref):
    i = pl.program_id(1)
    m_len = mem_ref.shape[1]

    @pl.when(i == 0)
    def _():
        hm = _rms(mem_ref[0], g_ref[...]).astype(BF16)
        k_s[...] = _dot(hm, wk_ref[...]).astype(BF16)
        vt = _dot_t(wvt_ref[...], hm)
        ones = _ones_rows(m_len)
        for h in range(N_HEADS):
            vt_s[h, 0:HEAD_DIM, :] = vt[h * HEAD_DIM:(h + 1) * HEAD_DIM, :].astype(BF16)
            vt_s[h, HEAD_DIM:VT_ROWS, :] = ones

    _masked_heads(q_ref, qm_ref, HEAD_DIM)
    outs = []
    for h in range(N_HEADS):
        g = h // 2
        st = _dot_t(k_s[:, g * LANES:(g + 1) * LANES], qm_ref[h])
        p = jnp.exp2(st - jnp.max(st, axis=0, keepdims=True)).astype(BF16)
        acc = _dot(vt_s[h], p)
        outs.append(acc[0:HEAD_DIM, :] / acc[HEAD_DIM:HEAD_DIM + 1, :])
    for g in range(N_HEADS // 2):
        _store_pair(o_ref, g, outs[2 * g], outs[2 * g + 1])


def _mem_attn(qm, mem, g, wk, wvt, *, batch, seq):
    n = batch * seq
    t = TILE
    nq = seq // t
    m_len, d = mem.shape[1], mem.shape[2]
    fixed = lambda b, i: (0, 0)
    return pl.pallas_call(
        _mem_kernel,
        grid=(batch, nq),
        in_specs=[pl.BlockSpec((t, 256), lambda b, i: (b * nq + i, 0)),
                  pl.BlockSpec((1, m_len, d), lambda b, i: (b, 0, 0)),
                  pl.BlockSpec((1, d), fixed), pl.BlockSpec((d, 256), fixed), pl.BlockSpec((256, d), fixed)],
        out_specs=pl.BlockSpec((t, 256), lambda b, i: (b * nq + i, 0)),
        out_shape=jax.ShapeDtypeStruct((n, 256), BF16),
        scratch_shapes=[pltpu.VMEM((m_len, 256), BF16), pltpu.VMEM((N_HEADS, VT_ROWS, m_len), BF16),
                        pltpu.VMEM((N_HEADS, t, LANES), BF16)],
        compiler_params=_cparams(("arbitrary", "arbitrary")),
        name="mem_attn",
    )(qm, mem, g, wk, wvt)


def _dsa_kernel(qi_ref, wit_ref, qb_ref, ki_ref, kb_ref, vbt_ref, o_ref,
                keys_ref, qim_ref, qbm_ref, m_ref, acc_ref, sta_ref, mxa_ref, stb_ref, mxb_ref,
                *, top_k, idx_bits):
    i = pl.program_id(1)
    t = qi_ref.shape[0]
    n_ch = i + 1

    lane_q = lax.broadcasted_iota(I32, (t, 2 * LANES), 1)
    qi = qi_ref[...]
    for h in range(IDX_HEADS):
        qim_ref[h] = jnp.where((lane_q >> 5) == h, qi, jnp.zeros_like(qi))
    wit = wit_ref[...]

    def score_step(js, masked=False):
        kcs = [ki_ref[pl.ds(pl.multiple_of(j * t, t), t), :] for j in js]
        scores = [jnp.zeros((t, t), F32) for _ in js]
        for h in range(IDX_HEADS):
            w_h = wit[h:h + 1, :]
            scores = [sc + jnp.maximum(_dot_t(kc, qim_ref[h]), 0.0) * w_h for sc, kc in zip(scores, kcs)]
        for j, score in zip(js, scores):
            bits = lax.bitcast_convert_type(score, I32)
            key = bits ^ ((bits >> 31) & jnp.int32(0x7FFFFFFF))
            key = jnp.where(key == -1, 0, key)
            if masked:
                key = jnp.where(_diag_mask(t), key, jnp.int32(INT_MIN))
            keys_ref[j] = key

    _chunk_pairs(i, score_step)
    score_step([i], masked=True)
    for pad in range(KEY_PAD):
        keys_ref[n_ch + pad] = jnp.full((t, t), INT_MIN, I32)
    n_pairs = (n_ch + 1) // 2

    def count(hits):
        def body(jj, acc):
            for j in (2 * jj, 2 * jj + 1):
                acc = acc + jnp.sum(hits(keys_ref[j], j).reshape(t // 8, 8, t), axis=0)
            return acc
        acc = lax.fori_loop(0, n_pairs, body, jnp.zeros((8, t), F32))
        return jnp.sum(acc, axis=0, keepdims=True)

    assert t >= top_k

    def group_max(jj, g):
        return jnp.maximum(g, jnp.maximum(keys_ref[2 * jj], keys_ref[2 * jj + 1]))

    g_max = lax.fori_loop(0, n_pairs, group_max, jnp.full((t, t), INT_MIN, I32))
    lo0 = jnp.min(g_max, axis=0, keepdims=True)
    top = jnp.max(g_max, axis=0, keepdims=True)
    hi0 = jnp.where(top < jnp.int32(2 ** 31 - 1), top + 1, top)
    n_valid = i * t + lax.broadcasted_iota(I32, (1, t), 1) + 1

    def midpoint(lo, hi):
        return (lo & hi) + ((lo ^ hi) >> 1)

    def unsettled(lo, hi, cnt_lo):
        open_ = jnp.where(cnt_lo != top_k, jnp.where(midpoint(lo, hi) != lo, 1.0, 0.0), 0.0)
        return jnp.max(jnp.where(n_valid > top_k, open_, 0.0)) > 0.0

    def bisect(carry):
        n, lo, hi, cnt_lo, _ = carry
        for _ in range(BITS_PER_CHECK):
            mid = midpoint(lo, hi)
            cnt = count(lambda kc, j, mid=mid: jnp.where(kc >= mid, 1.0, 0.0))
            take = cnt >= top_k
            cnt_lo = jnp.where(take, cnt, cnt_lo)
            lo = jnp.where(take, mid, lo)
            hi = jnp.where(take, hi, mid)
        return n + BITS_PER_CHECK, lo, hi, cnt_lo, unsettled(lo, hi, cnt_lo)

    cnt0 = count(lambda kc, j: jnp.where(kc >= lo0, 1.0, 0.0))
    n_pos = count(lambda kc, j: jnp.where(kc >= 1, 1.0, 0.0))
    n_nonneg = count(lambda kc, j: jnp.where(kc >= 0, 1.0, 0.0))
    pos = n_pos >= top_k
    zero = n_nonneg >= top_k
    lo1 = jnp.where(pos, jnp.maximum(lo0, 1), jnp.where(zero, 0, lo0))
    hi1 = jnp.where(pos, hi0, jnp.where(zero, 1, jnp.minimum(hi0, 0)))
    cnt1 = jnp.where(pos, jnp.where(lo0 >= 1, cnt0, n_pos), jnp.where(zero, n_nonneg, cnt0))
    _, thr, _, n_ge, _ = lax.while_loop(
        lambda c: jnp.logical_and(c[0] < 32 + BITS_PER_CHECK, c[4]), bisect,
        (jnp.int32(0), lo1, hi1, cnt1, unsettled(lo1, hi1, cnt1)))

    tie = jnp.where(thr > INT_MIN, jnp.where(n_ge > top_k, 1.0, 0.0), 0.0)

    @pl.when(jnp.max(tie) > 0.0)
    def _():
        n_gt = count(lambda kc, j: jnp.where(kc > thr, 1.0, 0.0))
        need = top_k - n_gt

        def key_pos(j):
            return j * t + lax.broadcasted_iota(I32, (t, t), 0)

        def idx_step(b, cut):
            cand = cut + (jnp.int32(1) << (idx_bits - 1 - b))
            cnt = count(lambda kc, j: jnp.where(kc == thr, jnp.where(key_pos(j) < cand, 1.0, 0.0), 0.0))
            return jnp.where(cnt < need, cand, cut)

        cut = lax.fori_loop(0, idx_bits, idx_step, jnp.zeros((1, t), I32))
        cut = jnp.where(tie > 0.0, cut, jnp.int32(2 ** 30))

        def demote(j, c):
            kc = keys_ref[j]
            demoted = jnp.where(key_pos(j) > cut, kc - 1, kc)
            keys_ref[j] = jnp.where(kc == thr, demoted, kc)
            return c

        lax.fori_loop(0, n_ch, demote, 0)

    thr_sel = jnp.maximum(thr, jnp.int32(INT_MIN + 1))

    _softmax_init(m_ref, acc_ref)
    _masked_heads(qb_ref, qbm_ref, HEAD_DIM)

    nq = vbt_ref.shape[0]

    def to_bias(j, c):
        bias = jnp.where(keys_ref[j] >= thr_sel, 0.0, NEG)
        keys_ref[j] = lax.bitcast_convert_type(bias, I32)
        return c

    lax.fori_loop(0, 2 * n_pairs, to_bias, 0)

    def chunks_of(n):
        return [(jnp.minimum(j, nq), jnp.minimum(j, nq - 1)) for j in (2 * n, 2 * n + 1)]

    def score_tiles(n):
        biases = [lax.bitcast_convert_type(keys_ref[jk], F32) for jk, _ in chunks_of(n)]
        ks = [kb_ref[pl.ds(pl.multiple_of(jv * t, t), t), :] for _, jv in chunks_of(n)]
        return [[_dot_t(k, qbm_ref[h]) + bias for bias, k in zip(biases, ks)] for h in range(N_HEADS)]

    def value_tiles(n):
        return [[vbt_ref[jv] for _, jv in chunks_of(n)]] * N_HEADS

    _pipelined_steps(n_pairs, score_tiles, value_tiles, list(range(N_HEADS)),
                     ((sta_ref, mxa_ref), (stb_ref, mxb_ref)), m_ref, acc_ref)
    for g in range(N_HEADS // 2):
        _store_pair(o_ref, g, _normalized(acc_ref, 2 * g), _normalized(acc_ref, 2 * g + 1))


def _dsa(qi, wit, qb, ki, kb, vbt, *, batch, seq):
    n = batch * seq
    t = TILE
    nq = seq // t
    top_k = min(IDX_TOPK_MAX, seq // 4)
    idx_bits = max(1, int(math.ceil(math.log2(seq))))
    qrow = lambda b, i: (b * nq + i, 0)
    kv = lambda b, i: (b, 0)
    return pl.pallas_call(
        functools.partial(_dsa_kernel, top_k=top_k, idx_bits=idx_bits),
        grid=(batch, nq),
        in_specs=[pl.BlockSpec((t, 256), qrow), pl.BlockSpec((BF16_ROWS, t), lambda b, i: (0, b * nq + i)),
                  pl.BlockSpec((t, 256), qrow),
                  pl.BlockSpec((seq, 256), kv), pl.BlockSpec((seq, LANES), kv),
                  pl.BlockSpec((nq, VT_ROWS, t), lambda b, i: (b, 0, 0))],
        out_specs=pl.BlockSpec((t, 256), qrow),
        out_shape=jax.ShapeDtypeStruct((n, 256), BF16),
        scratch_shapes=[pltpu.VMEM((nq + KEY_PAD, t, t), I32),
                        pltpu.VMEM((IDX_HEADS, t, 2 * LANES), BF16),
                        pltpu.VMEM((N_HEADS, t, LANES), BF16),
                        pltpu.VMEM((N_HEADS, 1, t), F32), pltpu.VMEM((N_HEADS, VT_ROWS, t), F32),
                        pltpu.VMEM((2 * N_HEADS, t, t), F32), pltpu.VMEM((N_HEADS, 1, t), F32),
                        pltpu.VMEM((2 * N_HEADS, t, t), F32), pltpu.VMEM((N_HEADS, 1, t), F32)],
        compiler_params=_cparams(("arbitrary", "arbitrary")),
        name="dsa",
    )(qi, wit, qb, ki, kb, vbt)


def _merge_kernel(x_ref, g_ref, oa_ref, ob_ref, oc_ref, om_ref, wg_ref, bg_ref, wb_ref, wo_ref, y_ref):
    x = x_ref[...]
    d = x.shape[1]
    h = _rms(x, g_ref[...]).astype(BF16)
    acc = jnp.zeros(x.shape, F32)
    for nb, o_ref in enumerate((oa_ref, ob_ref, oc_ref, om_ref)):
        gate = jax.nn.sigmoid(_dot(h, wg_ref[:, nb * d:(nb + 1) * d]) + bg_ref[:, nb * d:(nb + 1) * d])
        acc = acc + gate * _dot(o_ref[...], wb_ref[nb])
    y_ref[...] = x + _dot(acc.astype(BF16), wo_ref[...])


def _merge(x2, g, oa, ob, oc, om, wg, bg, wb, wo):
    n, d = x2.shape
    tm = TILE
    row = lambda i: (i, 0)
    fixed2 = lambda i: (0, 0)
    bw = oa.shape[1]
    return pl.pallas_call(
        _merge_kernel,
        grid=(n // tm,),
        in_specs=[pl.BlockSpec((tm, d), row), pl.BlockSpec((1, d), fixed2),
                  pl.BlockSpec((tm, bw), row), pl.BlockSpec((tm, bw), row),
                  pl.BlockSpec((tm, bw), row), pl.BlockSpec((tm, bw), row),
                  pl.BlockSpec((d, N_BRANCH * d), fixed2), pl.BlockSpec((1, N_BRANCH * d), fixed2),
                  pl.BlockSpec((N_BRANCH, bw, d), lambda i: (0, 0, 0)), pl.BlockSpec((d, d), fixed2)],
        out_specs=pl.BlockSpec((tm, d), row),
        out_shape=jax.ShapeDtypeStruct((n, d), F32),
        compiler_params=_cparams(("arbitrary",)),
        name="merge",
    )(x2, g, oa, ob, oc, om, wg, bg, wb, wo)


def _ffn_kernel(x_ref, g_ref, wi_ref, wo_ref, gf_ref, y_ref, *, hidden, chunk, final):
    x = x_ref[...]
    h = _rms(x, g_ref[...]).astype(BF16)
    acc = jnp.zeros(x.shape, F32)
    for c in range(hidden // chunk):
        a = _dot(h, wi_ref[:, c * chunk:(c + 1) * chunk])
        b = _dot(h, wi_ref[:, hidden + c * chunk:hidden + (c + 1) * chunk])
        t = (a * jax.nn.sigmoid(a)) * b
        acc = acc + _dot(t.astype(BF16), wo_ref[c * chunk:(c + 1) * chunk, :])
    y = x + acc
    if final:
        y = _rms(y, gf_ref[...])
    y_ref[...] = y


def _ffn(x2, g, w_in, w_out, g_final, *, final):
    n, d = x2.shape
    tm = TILE
    hidden = w_out.shape[0]
    chunk = hidden // 4 if (hidden // 4) % LANES == 0 else hidden
    row = lambda i: (i, 0)
    fixed2 = lambda i: (0, 0)
    return pl.pallas_call(
        functools.partial(_ffn_kernel, hidden=hidden, chunk=chunk, final=final),
        grid=(n // tm,),
        in_specs=[pl.BlockSpec((tm, d), row), pl.BlockSpec((1, d), fixed2),
                  pl.BlockSpec((d, 2 * hidden), fixed2), pl.BlockSpec((hidden, d), fixed2),
                  pl.BlockSpec((1, d), fixed2)],
        out_specs=pl.BlockSpec((tm, d), row),
        out_shape=jax.ShapeDtypeStruct((n, d), F32),
        compiler_params=_cparams(("arbitrary",)),
        name="ffn",
    )(x2, g, w_in, w_out, g_final)


_IN_SPLITS = (256, 256, 256, 4, 256, 64, 64, 256, 32, 8, 256, 256, 256, 256)


def _pack_w_in(w):
    d = w.shape[0]
    offs = np.cumsum((0,) + _IN_SPLITS)
    qa, ka, va, fa, qb, kb, vb, qi, ki, wi, qc, kc, vc, qm = (
        w[:, offs[k]:offs[k + 1]] for k in range(len(_IN_SPLITS)))

    def pad_heads(a):
        a = a.reshape(d, N_HEADS, HEAD_DIM)
        return jnp.pad(a, ((0, 0), (0, 0), (0, LANES - HEAD_DIM))).reshape(d, N_HEADS * LANES)

    def pad_to(a, wd):
        return jnp.pad(a, ((0, 0), (0, wd - a.shape[1])))

    cols = [pad_heads(qa), pad_heads(ka), pad_to(fa, LANES), qb, jnp.concatenate([kb, kb], axis=1),
            qi, jnp.tile(ki, (1, IDX_HEADS)), qc, kc, qm]
    rows = [va, vc, vb, pad_to(wi, BF16_ROWS)]
    return (jnp.concatenate(cols, axis=1).astype(BF16), jnp.concatenate(rows, axis=1).T.astype(BF16))


def _fox_consts():
    eq = np.zeros((3, LANES, 512), np.float32)
    ek = np.zeros((3, LANES, 512), np.float32)
    oq = np.zeros((1, 512), np.float32)
    ok = np.zeros((1, 512), np.float32)
    for h in range(N_HEADS):
        base = h * LANES + HEAD_DIM
        for t in range(3):
            eq[t, h, base + t] = 1.0
            ek[t, h, base + 3 + t] = -1.0
            oq[0, base + 3 + t] = 1.0
            ok[0, base + t] = 1.0
    return (jnp.asarray(eq, BF16), jnp.asarray(ek, BF16), jnp.asarray(oq), jnp.asarray(ok))


def _rope_tables(positions, rot_dim, period):
    half = rot_dim // 2
    inv_freq = ROPE_THETA ** (-jnp.arange(0, rot_dim, 2, dtype=F32) / rot_dim)
    ang = positions.astype(F32).reshape(-1, 1) * inv_freq
    cos, sin = jnp.cos(ang), jnp.sin(ang)
    n = ang.shape[0]
    c = jnp.concatenate([cos, cos, jnp.ones((n, period - 2 * half), F32)], axis=1)
    s = jnp.concatenate([-sin, sin, jnp.zeros((n, period - 2 * half), F32)], axis=1)
    reps = LANES // period
    return jnp.tile(c, (1, reps)), jnp.tile(s, (1, reps))


def kernel(x, mem, positions, norm_mix_g, w_in, b_forget, diff_lambda, diff_subln_g, norm_mem_g,
           w_mem_kv, w_branch, w_gate, b_gate, w_out, norm_ffn_g, w_ffn_in, w_ffn_out, final_norm_g):
    batch, seq, d = x.shape
    depth = w_in.shape[0]
    n = batch * seq
    assert seq % TILE == 0, seq

    c16, s16 = _rope_tables(positions, HEAD_DIM // 4, HEAD_DIM)
    c8, s8 = _rope_tables(positions, DIFF_QK_DIM // 4, DIFF_QK_DIM)
    consts = _fox_consts()

    x2 = x.reshape(n, d)
    for l in range(depth):
        lam_init = 0.8 - 0.6 * math.exp(-0.3 * l)
        bf = jnp.pad(b_forget[l], (0, LANES - b_forget.shape[1])).reshape(1, LANES)
        w_row, w_t = _pack_w_in(w_in[l])
        z = _proj(x2, norm_mix_g[l].reshape(1, d), w_row, w_t, bf, (c16, s16, c8, s8), consts, seq=seq)
        o_a = _fox(z["qa"], z["ka"], z["vaT"], batch=batch, seq=seq)
        o_b = _dsa(z["qi"], z["wiT"], z["qb"], z["ki"], z["kb"], z["vbT"], batch=batch, seq=seq)
        o_c = _diff(z["qc"], z["kc"], z["vcT"], diff_lambda[l], jnp.full((1, 1), lam_init, F32),
                    jnp.tile(diff_subln_g[l], 2).reshape(1, LANES), batch=batch, seq=seq)
        w_kv = w_mem_kv[l]
        o_m = _mem_attn(z["qm"], mem, norm_mem_g[l].reshape(1, d), w_kv[:, :256].astype(BF16),
                        w_kv[:, 256:].T.astype(BF16), batch=batch, seq=seq)
        x2 = _merge(x2, norm_mix_g[l].reshape(1, d), o_a, o_b, o_c, o_m,
                    w_gate[l].astype(BF16), b_gate[l].reshape(1, -1), w_branch[l].astype(BF16),
                    w_out[l].astype(BF16))
        x2 = _ffn(x2, norm_ffn_g[l].reshape(1, d), w_ffn_in[l].astype(BF16), w_ffn_out[l].astype(BF16),
                  final_norm_g.reshape(1, d), final=(l == depth - 1))
    return x2.reshape(batch, seq, d)
```

```python
import functools
import math

import numpy as np
import jax
import jax.numpy as jnp
from jax import lax
from jax.experimental import pallas as pl
from jax.experimental.pallas import tpu as pltpu

F32 = jnp.float32
BF16 = jnp.bfloat16
I32 = jnp.int32

HEAD_DIM = 64
N_BRANCH = 4
N_HEADS = 4
IDX_HEADS = 8
IDX_DIM = 32
IDX_TOPK_MAX = 256
DIFF_QK_DIM = 32
ROPE_THETA = 500000.0
NORM_EPS = 1e-6
LANES = 128
BF16_ROWS = 16
NEG = -1e30
LOG2E = math.log2(math.e)
INT_MIN = -2 ** 31
TILE = 256
DENSE_ROWS = 512
VT_ROWS = HEAD_DIM + BF16_ROWS
BITS_PER_CHECK = 4
KEY_PAD = 1

VMEM_LIMIT = 56 * 1024 * 1024

_SEG = {}
_off = 0
for _name, _w in (("qa", 512), ("ka", 512), ("fa", 128), ("qb", 256), ("kb", 128),
                  ("qi", 256), ("ki", 256), ("qc", 256), ("kc", 256), ("qm", 256)):
    _SEG[_name] = (_off, _off + _w)
    _off += _w
PROJ_COLS = _off
_TSEG = {"va": (0, 256), "vc": (256, 512), "vb": (512, 576), "wi": (576, 592)}
PROJ_T_ROWS = 592

_DN_T = (((1,), (1,)), ((), ()))


def _dot(a, b):
    return jnp.dot(a, b, preferred_element_type=F32)


def _dot_t(a, b):
    return lax.dot_general(a, b, _DN_T, preferred_element_type=F32)


def _rms(x, g):
    return x * lax.rsqrt(jnp.mean(x * x, axis=-1, keepdims=True) + NORM_EPS) * g


def _cparams(sem):
    return pltpu.CompilerParams(dimension_semantics=sem, vmem_limit_bytes=VMEM_LIMIT)


def _split3(x):
    hi = x.astype(BF16)
    r = x - hi.astype(F32)
    mid = r.astype(BF16)
    lo = (r - mid.astype(F32)).astype(BF16)
    return hi, mid, lo


def _ones_rows(cols):
    r = lax.broadcasted_iota(I32, (BF16_ROWS, cols), 0)
    return jnp.where(r == 0, 1.0, 0.0).astype(BF16)


def _rope(z, c, s, half, period):
    lane = lax.broadcasted_iota(I32, z.shape, 1)
    up = pltpu.roll(z, LANES - half, 1)
    dn = pltpu.roll(z, half, 1)
    partner = jnp.where((lane & (period - 1)) < half, up, dn)
    return z * c + partner * s


def _proj_kernel(x_ref, g_ref, w_ref, wt_ref, bf_ref, c16_ref, s16_ref, c8_ref, s8_ref, e_ref,
                 oq_ref, ok_ref,
                 qa_o, ka_o, qb_o, kb_o, qi_o, ki_o, qc_o, kc_o, qm_o, vat_o, vct_o, vbt_o, wit_o,
                 carry_ref, *, tiles_per_batch):
    i = pl.program_id(0)
    tm = x_ref.shape[0]
    h = _rms(x_ref[...], g_ref[...]).astype(BF16)

    def mm(name):
        lo, hi = _SEG[name]
        return _dot(h, w_ref[:, lo:hi])

    f = mm("fa") + bf_ref[...]
    ls = jnp.minimum(f, 0.0) - jnp.log(1.0 + jnp.exp(-jnp.abs(f)))
    r_i = lax.broadcasted_iota(I32, (tm, tm), 0)
    c_i = lax.broadcasted_iota(I32, (tm, tm), 1)
    ltri = jnp.where(c_i <= r_i, 1.0, 0.0).astype(BF16)
    p_hi, p_mid, p_lo = _split3(ls)
    pre = _dot(ltri, p_hi) + _dot(ltri, p_mid) + _dot(ltri, p_lo)

    @pl.when(i % tiles_per_batch == 0)
    def _():
        carry_ref[...] = jnp.zeros_like(carry_ref)

    cum = pre + carry_ref[...]
    carry_ref[...] = cum[tm - 1:tm, :]
    c_hi, c_mid, c_lo = _split3(cum * LOG2E)
    lane = lax.broadcasted_iota(I32, cum.shape, 1)
    c_terms = jnp.where(lane < N_HEADS, c_hi, jnp.where(lane < 2 * N_HEADS, c_mid, c_lo))

    placed = _dot(c_terms, e_ref[...])
    qa = mm("qa") * (HEAD_DIM ** -0.5 * LOG2E) + placed[:, :512] + oq_ref[...]
    qa_o[...] = qa.astype(BF16)
    ka = mm("ka") + placed[:, 512:] + ok_ref[...]
    ka_o[...] = ka.astype(BF16)

    c16 = c16_ref[...]
    s16 = s16_ref[...]
    c8 = c8_ref[...]
    s8 = s8_ref[...]

    def roped(name, out_ref, c, s, half, period, scale):
        z = mm(name)
        for k in range(z.shape[1] // LANES):
            r = _rope(z[:, k * LANES:(k + 1) * LANES], c, s, half, period)
            if scale != 1.0:
                r = r * scale
            out_ref[:, k * LANES:(k + 1) * LANES] = r.astype(BF16)

    roped("qb", qb_o, c16, s16, 8, 64, HEAD_DIM ** -0.5 * LOG2E)
    roped("kb", kb_o, c16, s16, 8, 64, 1.0)
    roped("qi", qi_o, c8, s8, 4, 32, 1.0)
    roped("ki", ki_o, c8, s8, 4, 32, 1.0)
    roped("qc", qc_o, c8, s8, 4, 32, DIFF_QK_DIM ** -0.5 * LOG2E)
    roped("kc", kc_o, c8, s8, 4, 32, 1.0)
    qm_o[...] = (mm("qm") * (HEAD_DIM ** -0.5 * LOG2E)).astype(BF16)

    zt = _dot_t(wt_ref[...], h)
    ones = _ones_rows(tm)
    for name, out_ref, heads in (("va", vat_o, N_HEADS), ("vc", vct_o, N_HEADS), ("vb", vbt_o, 1)):
        lo = _TSEG[name][0]
        for hd in range(heads):
            out_ref[0, hd * VT_ROWS:hd * VT_ROWS + HEAD_DIM, :] = (
                zt[lo + hd * HEAD_DIM:lo + (hd + 1) * HEAD_DIM, :].astype(BF16))
            out_ref[0, hd * VT_ROWS + HEAD_DIM:(hd + 1) * VT_ROWS, :] = ones
    lo, hi = _TSEG["wi"]
    wit_o[...] = zt[lo:hi, :] * (1.0 / 16.0)


def _proj(x2, g, w, wt, bf, tabs, consts, *, seq):
    n, d = x2.shape
    tm = TILE
    grid = (n // tm,)
    row = lambda i: (i, 0)
    fixed2 = lambda i: (0, 0)
    chunk3 = lambda i: (i, 0, 0)
    e_place, oq, ok = consts
    names = ("qa", "ka", "qb", "kb", "qi", "ki", "qc", "kc", "qm")
    out_specs = [pl.BlockSpec((tm, _SEG[nm][1] - _SEG[nm][0]), row) for nm in names]
    out_shape = [jax.ShapeDtypeStruct((n, _SEG[nm][1] - _SEG[nm][0]), BF16) for nm in names]
    for heads in (N_HEADS, N_HEADS, 1):
        out_specs.append(pl.BlockSpec((1, heads * VT_ROWS, tm), chunk3))
        out_shape.append(jax.ShapeDtypeStruct((n // tm, heads * VT_ROWS, tm), BF16))
    out_specs.append(pl.BlockSpec((BF16_ROWS, tm), lambda i: (0, i)))
    out_shape.append(jax.ShapeDtypeStruct((BF16_ROWS, n), F32))
    outs = pl.pallas_call(
        functools.partial(_proj_kernel, tiles_per_batch=seq // tm),
        grid=grid,
        in_specs=[pl.BlockSpec((tm, d), row), pl.BlockSpec((1, d), fixed2),
                  pl.BlockSpec((d, PROJ_COLS), fixed2), pl.BlockSpec((PROJ_T_ROWS, d), fixed2),
                  pl.BlockSpec((1, LANES), fixed2),
                  pl.BlockSpec((tm, LANES), row), pl.BlockSpec((tm, LANES), row),
                  pl.BlockSpec((tm, LANES), row), pl.BlockSpec((tm, LANES), row),
                  pl.BlockSpec((LANES, 1024), fixed2),
                  pl.BlockSpec((1, 512), fixed2), pl.BlockSpec((1, 512), fixed2)],
        out_specs=out_specs,
        out_shape=out_shape,
        scratch_shapes=[pltpu.VMEM((1, LANES), F32)],
        compiler_params=_cparams(("arbitrary",)),
        name="proj",
    )(x2, g, w, wt, bf, *tabs, e_place, oq, ok)
    return dict(zip(names + ("vaT", "vcT", "vbT", "wiT"), outs))


def _softmax_init(m_ref, acc_ref):
    m_ref[...] = jnp.full(m_ref.shape, NEG, F32)
    acc_ref[...] = jnp.zeros(acc_ref.shape, F32)


def _softmax_steps(streams, m_ref, acc_ref):
    n = len(streams)
    sts, alphas, ps = [None] * n, [None] * n, [None] * n

    def scores(k):
        sts[k] = streams[k][1]()

    def probs(k):
        idx = streams[k][0]
        m_prev = m_ref[idx]
        m_new = m_prev
        for st in sts[k]:
            m_new = jnp.maximum(m_new, jnp.max(st, axis=0, keepdims=True))
        alphas[k] = jnp.exp2(m_prev - m_new)
        ps[k] = [jnp.exp2(st - m_new).astype(BF16) for st in sts[k]]
        m_ref[idx] = m_new

    def values(k):
        idx, _, vts = streams[k]
        pv = _dot(vts[0], ps[k][0])
        for vt, p in zip(vts[1:], ps[k][1:]):
            pv = pv + _dot(vt, p)
        acc_ref[idx] = alphas[k] * acc_ref[idx] + pv

    for stage in (scores, probs, values):
        for k in range(n):
            stage(k)


def _chunk_pairs(n_full, step):
    def body(jj, c):
        step([2 * jj, 2 * jj + 1])
        return c

    lax.fori_loop(0, n_full // 2, body, 0)

    @pl.when(n_full % 2 == 1)
    def _():
        step([n_full - 1])


def _pipelined_steps(n_steps, score_tiles, value_tiles, idxs, bufs, m_ref, acc_ref):
    def produce(n, st_buf, mx_buf):
        tiles = score_tiles(n)
        for k, sts in enumerate(tiles):
            mx = None
            for c, st in enumerate(sts):
                st_buf[k * len(sts) + c] = st
                cm = jnp.max(st, axis=0, keepdims=True)
                mx = cm if mx is None else jnp.maximum(mx, cm)
            mx_buf[k] = mx

    def consume(n, st_buf, mx_buf):
        vts = value_tiles(n)
        n_c = len(vts[0])
        alphas, ps = [], []
        for k, idx in enumerate(idxs):
            m_prev = m_ref[idx]
            m_new = jnp.maximum(m_prev, mx_buf[k])
            alphas.append(jnp.exp2(m_prev - m_new))
            ps.append([jnp.exp2(st_buf[k * n_c + c] - m_new).astype(BF16) for c in range(n_c)])
            m_ref[idx] = m_new
        for k, idx in enumerate(idxs):
            pv = _dot(vts[k][0], ps[k][0])
            for c in range(1, n_c):
                pv = pv + _dot(vts[k][c], ps[k][c])
            acc_ref[idx] = alphas[k] * acc_ref[idx] + pv

    (st_a, mx_a), (st_b, mx_b) = bufs

    @pl.when(n_steps > 0)
    def _():
        produce(0, st_a, mx_a)

    def body(s, c):
        produce(2 * s + 1, st_b, mx_b)
        consume(2 * s, st_a, mx_a)
        produce(2 * s + 2, st_a, mx_a)
        consume(2 * s + 1, st_b, mx_b)
        return c

    lax.fori_loop(0, n_steps // 2, body, 0)

    @pl.when(n_steps % 2 == 1)
    def _():
        consume(n_steps - 1, st_a, mx_a)


def _normalized(acc_ref, idx):
    acc = acc_ref[idx]
    return acc[0:HEAD_DIM, :] / acc[HEAD_DIM:HEAD_DIM + 1, :]


def _store_pair(o_ref, g, o0, o1, scale_row=None):
    o = jnp.concatenate([o0, o1], axis=0).T
    if scale_row is not None:
        o = o * scale_row
    o_ref[:, g * LANES:(g + 1) * LANES] = o.astype(o_ref.dtype)


def _diag_mask(t):
    key = lax.broadcasted_iota(I32, (t, t), 0)
    qry = lax.broadcasted_iota(I32, (t, t), 1)
    return key <= qry


def _masked_heads(q_ref, qm_ref, width):
    tq = q_ref.shape[0]
    lane = lax.broadcasted_iota(I32, (tq, LANES), 1)
    per_group = LANES // width
    for idx in range(2 * per_group):
        g = idx // per_group
        qg = q_ref[:, g * LANES:(g + 1) * LANES]
        lo = (idx % per_group) * width
        sel = (lane >= lo) & (lane < lo + width)
        qm_ref[idx] = jnp.where(sel, qg, jnp.zeros_like(qg))


def _fox_kernel(q_ref, k_ref, vt_ref, o_ref, m_ref, acc_ref, sta_ref, mxa_ref, stb_ref, mxb_ref):
    i = pl.program_id(1)
    t = q_ref.shape[0]
    nq = vt_ref.shape[0]
    _softmax_init(m_ref, acc_ref)

    def chunks_of(n):
        return [jnp.minimum(j, nq - 1) for j in (2 * n, 2 * n + 1)]

    def score_tiles(n):
        return [[_dot_t(k_ref[pl.ds(pl.multiple_of(j * t, t), t), h * LANES:(h + 1) * LANES],
                        q_ref[:, h * LANES:(h + 1) * LANES]) for j in chunks_of(n)] for h in range(N_HEADS)]

    def value_tiles(n):
        return [[vt_ref[j, h * VT_ROWS:(h + 1) * VT_ROWS, :] for j in chunks_of(n)] for h in range(N_HEADS)]

    _pipelined_steps(i // 2, score_tiles, value_tiles, list(range(N_HEADS)),
                     ((sta_ref, mxa_ref), (stb_ref, mxb_ref)), m_ref, acc_ref)

    def step(js, masked=False):
        streams = []
        for h in range(N_HEADS):
            def scores(h=h):
                q = q_ref[:, h * LANES:(h + 1) * LANES]
                sts = [_dot_t(k_ref[pl.ds(pl.multiple_of(j * t, t), t), h * LANES:(h + 1) * LANES], q)
                       for j in js]
                if masked:
                    sts = [jnp.where(_diag_mask(t), st, NEG) for st in sts]
                return sts
            streams.append((h, scores, [vt_ref[j, h * VT_ROWS:(h + 1) * VT_ROWS, :] for j in js]))
        _softmax_steps(streams, m_ref, acc_ref)

    @pl.when(i % 2 == 1)
    def _():
        step([i - 1])

    step([i], masked=True)
    for g in range(N_HEADS // 2):
        _store_pair(o_ref, g, _normalized(acc_ref, 2 * g), _normalized(acc_ref, 2 * g + 1))


def _fox(qa, ka, vat, *, batch, seq):
    n = batch * seq
    t = TILE
    nq = seq // t
    return pl.pallas_call(
        _fox_kernel,
        grid=(batch, nq),
        in_specs=[pl.BlockSpec((t, 512), lambda b, i: (b * nq + i, 0)),
                  pl.BlockSpec((seq, 512), lambda b, i: (b, 0)),
                  pl.BlockSpec((nq, N_HEADS * VT_ROWS, t), lambda b, i: (b, 0, 0))],
        out_specs=pl.BlockSpec((t, 256), lambda b, i: (b * nq + i, 0)),
        out_shape=jax.ShapeDtypeStruct((n, 256), BF16),
        scratch_shapes=[pltpu.VMEM((N_HEADS, 1, t), F32), pltpu.VMEM((N_HEADS, VT_ROWS, t), F32),
                        pltpu.VMEM((2 * N_HEADS, t, t), F32), pltpu.VMEM((N_HEADS, 1, t), F32),
                        pltpu.VMEM((2 * N_HEADS, t, t), F32), pltpu.VMEM((N_HEADS, 1, t), F32)],
        compiler_params=_cparams(("arbitrary", "arbitrary")),
        name="fox",
    )(qa, ka, vat)


def _diff_kernel(q_ref, k_ref, vt_ref, dl_ref, li_ref, g_ref, o_ref, qm_ref, m_ref, acc_ref,
                 sta_ref, mxa_ref, stb_ref, mxb_ref):
    i = pl.program_id(1)
    t = q_ref.shape[0]
    nq = vt_ref.shape[0]
    n_streams = 2 * N_HEADS
    _softmax_init(m_ref, acc_ref)
    _masked_heads(q_ref, qm_ref, DIFF_QK_DIM)

    def chunks_of(n):
        return [jnp.minimum(j, nq - 1) for j in (2 * n, 2 * n + 1)]

    def score_tiles(n):
        tiles = []
        for g in range(N_HEADS // 2):
            ks = [k_ref[pl.ds(pl.multiple_of(j * t, t), t), g * LANES:(g + 1) * LANES] for j in chunks_of(n)]
            tiles += [[_dot_t(k, qm_ref[idx]) for k in ks] for idx in range(4 * g, 4 * g + 4)]
        return tiles

    def value_tiles(n):
        return [[vt_ref[j, (idx // 2) * VT_ROWS:(idx // 2 + 1) * VT_ROWS, :] for j in chunks_of(n)]
                for idx in range(n_streams)]

    _pipelined_steps(i // 2, score_tiles, value_tiles, list(range(n_streams)),
                     ((sta_ref, mxa_ref), (stb_ref, mxb_ref)), m_ref, acc_ref)

    def step(js, masked=False):
        for g in range(N_HEADS // 2):
            ks = [k_ref[pl.ds(pl.multiple_of(j * t, t), t), g * LANES:(g + 1) * LANES] for j in js]
            streams = []
            for idx in range(4 * g, 4 * g + 4):
                def scores(idx=idx, ks=ks):
                    sts = [_dot_t(k, qm_ref[idx]) for k in ks]
                    if masked:
                        sts = [jnp.where(_diag_mask(t), st, NEG) for st in sts]
                    return sts
                h = idx // 2
                streams.append((idx, scores, [vt_ref[j, h * VT_ROWS:(h + 1) * VT_ROWS, :] for j in js]))
            _softmax_steps(streams, m_ref, acc_ref)

    @pl.when(i % 2 == 1)
    def _():
        step([i - 1])

    step([i], masked=True)

    dl = dl_ref[...]
    lam_init = li_ref[...]
    lam = (jnp.exp(jnp.sum(dl[0:1] * dl[1:2], axis=-1, keepdims=True))
           - jnp.exp(jnp.sum(dl[2:3] * dl[3:4], axis=-1, keepdims=True)) + lam_init)
    scale_row = g_ref[...] * (1.0 - lam_init)
    for g in range(N_HEADS // 2):
        outs = []
        for h in (2 * g, 2 * g + 1):
            o = _normalized(acc_ref, 2 * h) - lam * _normalized(acc_ref, 2 * h + 1)
            outs.append(o * lax.rsqrt(jnp.mean(o * o, axis=0, keepdims=True) + NORM_EPS))
        _store_pair(o_ref, g, outs[0], outs[1], scale_row)


def _diff(qc, kc, vct, dl, lam_init, g2, *, batch, seq):
    n = batch * seq
    t = TILE
    nq = seq // t
    fixed = lambda b, i: (0, 0)
    return pl.pallas_call(
        _diff_kernel,
        grid=(batch, nq),
        in_specs=[pl.BlockSpec((t, 256), lambda b, i: (b * nq + i, 0)),
                  pl.BlockSpec((seq, 256), lambda b, i: (b, 0)),
                  pl.BlockSpec((nq, N_HEADS * VT_ROWS, t), lambda b, i: (b, 0, 0)),
                  pl.BlockSpec((4, DIFF_QK_DIM), fixed), pl.BlockSpec((1, 1), fixed),
                  pl.BlockSpec((1, LANES), fixed)],
        out_specs=pl.BlockSpec((t, 256), lambda b, i: (b * nq + i, 0)),
        out_shape=jax.ShapeDtypeStruct((n, 256), BF16),
        scratch_shapes=[pltpu.VMEM((2 * N_HEADS, t, LANES), BF16),
                        pltpu.VMEM((2 * N_HEADS, 1, t), F32),
                        pltpu.VMEM((2 * N_HEADS, VT_ROWS, t), F32),
                        pltpu.VMEM((4 * N_HEADS, t, t), F32), pltpu.VMEM((2 * N_HEADS, 1, t), F32),
                        pltpu.VMEM((4 * N_HEADS, t, t), F32), pltpu.VMEM((2 * N_HEADS, 1, t), F32)],
        compiler_params=_cparams(("arbitrary", "arbitrary")),
        name="diff",
    )(qc, kc, vct, dl, lam_init, g2)


def _mem_kernel(q_ref, mem_ref, g_ref, wk_ref, wvt_ref, o_ref, k_s, vt_s, qm_ref):
    i = pl.program_id(1)
    m_len = mem_ref.shape[1]

    @pl.when(i == 0)
    def _():
        hm = _rms(mem_ref[0], g_ref[...]).astype(BF16)
        k_s[...] = _dot(hm, wk_ref[...]).astype(BF16)
        vt = _dot_t(wvt_ref[...], hm)
        ones = _ones_rows(m_len)
        for h in range(N_HEADS):
            vt_s[h, 0:HEAD_DIM, :] = vt[h * HEAD_DIM:(h + 1) * HEAD_DIM, :].astype(BF16)
            vt_s[h, HEAD_DIM:VT_ROWS, :] = ones

    _masked_heads(q_ref, qm_ref, HEAD_DIM)
    sts = [_dot_t(k_s[:, (h // 2) * LANES:(h // 2 + 1) * LANES], qm_ref[h]) for h in range(N_HEADS)]
    ps = [jnp.exp2(st - jnp.max(st, axis=0, keepdims=True)).astype(BF16) for st in sts]
    accs = [_dot(vt_s[h], ps[h]) for h in range(N_HEADS)]
    outs = [acc[0:HEAD_DIM, :] / acc[HEAD_DIM:HEAD_DIM + 1, :] for acc in accs]
    for g in range(N_HEADS // 2):
        _store_pair(o_ref, g, outs[2 * g], outs[2 * g + 1])


def _mem_attn(qm, mem, g, wk, wvt, *, batch, seq):
    n = batch * seq
    t = TILE
    nq = seq // t
    m_len, d = mem.shape[1], mem.shape[2]
    fixed = lambda b, i: (0, 0)
    return pl.pallas_call(
        _mem_kernel,
        grid=(batch, nq),
        in_specs=[pl.BlockSpec((t, 256), lambda b, i: (b * nq + i, 0)),
                  pl.BlockSpec((1, m_len, d), lambda b, i: (b, 0, 0)),
                  pl.BlockSpec((1, d), fixed), pl.BlockSpec((d, 256), fixed), pl.BlockSpec((256, d), fixed)],
        out_specs=pl.BlockSpec((t, 256), lambda b, i: (b * nq + i, 0)),
        out_shape=jax.ShapeDtypeStruct((n, 256), BF16),
        scratch_shapes=[pltpu.VMEM((m_len, 256), BF16), pltpu.VMEM((N_HEADS, VT_ROWS, m_len), BF16),
                        pltpu.VMEM((N_HEADS, t, LANES), BF16)],
        compiler_params=_cparams(("arbitrary", "arbitrary")),
        name="mem_attn",
    )(qm, mem, g, wk, wvt)


def _dsa_kernel(qi_ref, wit_ref, qb_ref, ki_ref, kb_ref, vbt_ref, o_ref,
                keys_ref, qim_ref, qbm_ref, m_ref, acc_ref, sta_ref, mxa_ref, stb_ref, mxb_ref,
                *, top_k, idx_bits):
    i = pl.program_id(1)
    t = qi_ref.shape[0]
    n_ch = i + 1

    lane_q = lax.broadcasted_iota(I32, (t, 2 * LANES), 1)
    qi = qi_ref[...]
    for h in range(IDX_HEADS):
        qim_ref[h] = jnp.where((lane_q >> 5) == h, qi, jnp.zeros_like(qi))
    wit = wit_ref[...]

    def score_step(js, masked=False):
        kcs = [ki_ref[pl.ds(pl.multiple_of(j * t, t), t), :] for j in js]
        scores = [jnp.zeros((t, t), F32) for _ in js]
        for h in range(IDX_HEADS):
            w_h = wit[h:h + 1, :]
            scores = [sc + jnp.maximum(_dot_t(kc, qim_ref[h]), 0.0) * w_h for sc, kc in zip(scores, kcs)]
        for j, score in zip(js, scores):
            bits = lax.bitcast_convert_type(score, I32)
            key = bits ^ ((bits >> 31) & jnp.int32(0x7FFFFFFF))
            key = jnp.where(key == -1, 0, key)
            if masked:
                key = jnp.where(_diag_mask(t), key, jnp.int32(INT_MIN))
            keys_ref[j] = key

    _chunk_pairs(i, score_step)
    score_step([i], masked=True)
    for pad in range(KEY_PAD):
        keys_ref[n_ch + pad] = jnp.full((t, t), INT_MIN, I32)
    n_pairs = (n_ch + 1) // 2

    def count(hits):
        def body(jj, acc):
            for j in (2 * jj, 2 * jj + 1):
                acc = acc + jnp.sum(hits(keys_ref[j], j).reshape(t // 8, 8, t), axis=0)
            return acc
        acc = lax.fori_loop(0, n_pairs, body, jnp.zeros((8, t), F32))
        return jnp.sum(acc, axis=0, keepdims=True)

    assert t >= top_k

    def group_max(jj, g):
        return jnp.maximum(g, jnp.maximum(keys_ref[2 * jj], keys_ref[2 * jj + 1]))

    g_max = lax.fori_loop(0, n_pairs, group_max, jnp.full((t, t), INT_MIN, I32))
    lo0 = jnp.min(g_max, axis=0, keepdims=True)
    top = jnp.max(g_max, axis=0, keepdims=True)
    hi0 = jnp.where(top < jnp.int32(2 ** 31 - 1), top + 1, top)
    n_valid = i * t + lax.broadcasted_iota(I32, (1, t), 1) + 1

    def midpoint(lo, hi):
        return (lo & hi) + ((lo ^ hi) >> 1)

    def unsettled(lo, hi, cnt_lo):
        open_ = jnp.where(cnt_lo != top_k, jnp.where(midpoint(lo, hi) != lo, 1.0, 0.0), 0.0)
        return jnp.max(jnp.where(n_valid > top_k, open_, 0.0)) > 0.0

    def bisect(carry):
        n, lo, hi, cnt_lo, _ = carry
        for _ in range(BITS_PER_CHECK):
            mid = midpoint(lo, hi)
            cnt = count(lambda kc, j, mid=mid: jnp.where(kc >= mid, 1.0, 0.0))
            take = cnt >= top_k
            cnt_lo = jnp.where(take, cnt, cnt_lo)
            lo = jnp.where(take, mid, lo)
            hi = jnp.where(take, hi, mid)
        return n + BITS_PER_CHECK, lo, hi, cnt_lo, unsettled(lo, hi, cnt_lo)

    cnt0 = count(lambda kc, j: jnp.where(kc >= lo0, 1.0, 0.0))
    n_pos = count(lambda kc, j: jnp.where(kc >= 1, 1.0, 0.0))
    n_nonneg = count(lambda kc, j: jnp.where(kc >= 0, 1.0, 0.0))
    pos = n_pos >= top_k
    zero = n_nonneg >= top_k
    lo1 = jnp.where(pos, jnp.maximum(lo0, 1), jnp.where(zero, 0, lo0))
    hi1 = jnp.where(pos, hi0, jnp.where(zero, 1, jnp.minimum(hi0, 0)))
    cnt1 = jnp.where(pos, jnp.where(lo0 >= 1, cnt0, n_pos), jnp.where(zero, n_nonneg, cnt0))
    _, thr, _, n_ge, _ = lax.while_loop(
        lambda c: jnp.logical_and(c[0] < 32 + BITS_PER_CHECK, c[4]), bisect,
        (jnp.int32(0), lo1, hi1, cnt1, unsettled(lo1, hi1, cnt1)))

    tie = jnp.where(thr > INT_MIN, jnp.where(n_ge > top_k, 1.0, 0.0), 0.0)

    @pl.when(jnp.max(tie) > 0.0)
    def _():
        n_gt = count(lambda kc, j: jnp.where(kc > thr, 1.0, 0.0))
        need = top_k - n_gt

        def key_pos(j):
            return j * t + lax.broadcasted_iota(I32, (t, t), 0)

        def idx_step(b, cut):
            cand = cut + (jnp.int32(1) << (idx_bits - 1 - b))
            cnt = count(lambda kc, j: jnp.where(kc == thr, jnp.where(key_pos(j) < cand, 1.0, 0.0), 0.0))
            return jnp.where(cnt < need, cand, cut)

        cut = lax.fori_loop(0, idx_bits, idx_step, jnp.zeros((1, t), I32))
        cut = jnp.where(tie > 0.0, cut, jnp.int32(2 ** 30))

        def demote(j, c):
            kc = keys_ref[j]
            demoted = jnp.where(key_pos(j) > cut, kc - 1, kc)
            keys_ref[j] = jnp.where(kc == thr, demoted, kc)
            return c

        lax.fori_loop(0, n_ch, demote, 0)

    thr_sel = jnp.maximum(thr, jnp.int32(INT_MIN + 1))

    _softmax_init(m_ref, acc_ref)
    _masked_heads(qb_ref, qbm_ref, HEAD_DIM)

    nq = vbt_ref.shape[0]

    def to_bias(j, c):
        bias = jnp.where(keys_ref[j] >= thr_sel, 0.0, NEG)
        keys_ref[j] = lax.bitcast_convert_type(bias, I32)
        return c

    lax.fori_loop(0, 2 * n_pairs, to_bias, 0)

    def chunks_of(n):
        return [(jnp.minimum(j, nq), jnp.minimum(j, nq - 1)) for j in (2 * n, 2 * n + 1)]

    def score_tiles(n):
        biases = [lax.bitcast_convert_type(keys_ref[jk], F32) for jk, _ in chunks_of(n)]
        ks = [kb_ref[pl.ds(pl.multiple_of(jv * t, t), t), :] for _, jv in chunks_of(n)]
        return [[_dot_t(k, qbm_ref[h]) + bias for bias, k in zip(biases, ks)] for h in range(N_HEADS)]

    def value_tiles(n):
        return [[vbt_ref[jv] for _, jv in chunks_of(n)]] * N_HEADS

    _pipelined_steps(n_pairs, score_tiles, value_tiles, list(range(N_HEADS)),
                     ((sta_ref, mxa_ref), (stb_ref, mxb_ref)), m_ref, acc_ref)
    for g in range(N_HEADS // 2):
        _store_pair(o_ref, g, _normalized(acc_ref, 2 * g), _normalized(acc_ref, 2 * g + 1))


def _dsa(qi, wit, qb, ki, kb, vbt, *, batch, seq):
    n = batch * seq
    t = TILE
    nq = seq // t
    top_k = min(IDX_TOPK_MAX, seq // 4)
    idx_bits = max(1, int(math.ceil(math.log2(seq))))
    qrow = lambda b, i: (b * nq + i, 0)
    kv = lambda b, i: (b, 0)
    return pl.pallas_call(
        functools.partial(_dsa_kernel, top_k=top_k, idx_bits=idx_bits),
        grid=(batch, nq),
        in_specs=[pl.BlockSpec((t, 256), qrow), pl.BlockSpec((BF16_ROWS, t), lambda b, i: (0, b * nq + i)),
                  pl.BlockSpec((t, 256), qrow),
                  pl.BlockSpec((seq, 256), kv), pl.BlockSpec((seq, LANES), kv),
                  pl.BlockSpec((nq, VT_ROWS, t), lambda b, i: (b, 0, 0))],
        out_specs=pl.BlockSpec((t, 256), qrow),
        out_shape=jax.ShapeDtypeStruct((n, 256), BF16),
        scratch_shapes=[pltpu.VMEM((nq + KEY_PAD, t, t), I32),
                        pltpu.VMEM((IDX_HEADS, t, 2 * LANES), BF16),
                        pltpu.VMEM((N_HEADS, t, LANES), BF16),
                        pltpu.VMEM((N_HEADS, 1, t), F32), pltpu.VMEM((N_HEADS, VT_ROWS, t), F32),
                        pltpu.VMEM((2 * N_HEADS, t, t), F32), pltpu.VMEM((N_HEADS, 1, t), F32),
                        pltpu.VMEM((2 * N_HEADS, t, t), F32), pltpu.VMEM((N_HEADS, 1, t), F32)],
        compiler_params=_cparams(("arbitrary", "arbitrary")),
        name="dsa",
    )(qi, wit, qb, ki, kb, vbt)


def _merge_kernel(x_ref, g_ref, oa_ref, ob_ref, oc_ref, om_ref, wg_ref, bg_ref, wb_ref, wo_ref, y_ref):
    x = x_ref[...]
    d = x.shape[1]
    h = _rms(x, g_ref[...]).astype(BF16)
    acc = jnp.zeros(x.shape, F32)
    for nb, o_ref in enumerate((oa_ref, ob_ref, oc_ref, om_ref)):
        gate = jax.nn.sigmoid(_dot(h, wg_ref[:, nb * d:(nb + 1) * d]) + bg_ref[:, nb * d:(nb + 1) * d])
        acc = acc + gate * _dot(o_ref[...], wb_ref[nb])
    y_ref[...] = x + _dot(acc.astype(BF16), wo_ref[...])


def _merge(x2, g, oa, ob, oc, om, wg, bg, wb, wo):
    n, d = x2.shape
    tm = min(DENSE_ROWS, n)
    row = lambda i: (i, 0)
    fixed2 = lambda i: (0, 0)
    bw = oa.shape[1]
    once = pl.Buffered(1)
    return pl.pallas_call(
        _merge_kernel,
        grid=(n // tm,),
        in_specs=[pl.BlockSpec((tm, d), row), pl.BlockSpec((1, d), fixed2),
                  pl.BlockSpec((tm, bw), row), pl.BlockSpec((tm, bw), row),
                  pl.BlockSpec((tm, bw), row), pl.BlockSpec((tm, bw), row),
                  pl.BlockSpec((d, N_BRANCH * d), fixed2, pipeline_mode=once),
                  pl.BlockSpec((1, N_BRANCH * d), fixed2),
                  pl.BlockSpec((N_BRANCH, bw, d), lambda i: (0, 0, 0), pipeline_mode=once),
                  pl.BlockSpec((d, d), fixed2, pipeline_mode=once)],
        out_specs=pl.BlockSpec((tm, d), row),
        out_shape=jax.ShapeDtypeStruct((n, d), F32),
        compiler_params=_cparams(("arbitrary",)),
        name="merge",
    )(x2, g, oa, ob, oc, om, wg, bg, wb, wo)


def _ffn_kernel(x_ref, g_ref, wi_ref, wo_ref, gf_ref, y_ref, *, hidden, chunk, final):
    x = x_ref[...]
    h = _rms(x, g_ref[...]).astype(BF16)
    acc = jnp.zeros(x.shape, F32)
    for c in range(hidden // chunk):
        a = _dot(h, wi_ref[:, c * chunk:(c + 1) * chunk])
        b = _dot(h, wi_ref[:, hidden + c * chunk:hidden + (c + 1) * chunk])
        t = (a * jax.nn.sigmoid(a)) * b
        acc = acc + _dot(t.astype(BF16), wo_ref[c * chunk:(c + 1) * chunk, :])
    y = x + acc
    if final:
        y = _rms(y, gf_ref[...])
    y_ref[...] = y


def _ffn(x2, g, w_in, w_out, g_final, *, final):
    n, d = x2.shape
    tm = min(DENSE_ROWS, n)
    hidden = w_out.shape[0]
    chunk = hidden // 4 if (hidden // 4) % LANES == 0 else hidden
    row = lambda i: (i, 0)
    fixed2 = lambda i: (0, 0)
    once = pl.Buffered(1)
    return pl.pallas_call(
        functools.partial(_ffn_kernel, hidden=hidden, chunk=chunk, final=final),
        grid=(n // tm,),
        in_specs=[pl.BlockSpec((tm, d), row), pl.BlockSpec((1, d), fixed2),
                  pl.BlockSpec((d, 2 * hidden), fixed2, pipeline_mode=once),
                  pl.BlockSpec((hidden, d), fixed2, pipeline_mode=once),
                  pl.BlockSpec((1, d), fixed2)],
        out_specs=pl.BlockSpec((tm, d), row),
        out_shape=jax.ShapeDtypeStruct((n, d), F32),
        compiler_params=_cparams(("arbitrary",)),
        name="ffn",
    )(x2, g, w_in, w_out, g_final)


_IN_SPLITS = (256, 256, 256, 4, 256, 64, 64, 256, 32, 8, 256, 256, 256, 256)


def _pack_w_in(w):
    d = w.shape[0]
    offs = np.cumsum((0,) + _IN_SPLITS)
    qa, ka, va, fa, qb, kb, vb, qi, ki, wi, qc, kc, vc, qm = (
        w[:, offs[k]:offs[k + 1]] for k in range(len(_IN_SPLITS)))

    def pad_heads(a):
        a = a.reshape(d, N_HEADS, HEAD_DIM)
        return jnp.pad(a, ((0, 0), (0, 0), (0, LANES - HEAD_DIM))).reshape(d, N_HEADS * LANES)

    def pad_to(a, wd):
        return jnp.pad(a, ((0, 0), (0, wd - a.shape[1])))

    cols = [pad_heads(qa), pad_heads(ka), pad_to(jnp.tile(fa, (1, 3)), LANES), qb,
            jnp.concatenate([kb, kb], axis=1),
            qi, jnp.tile(ki, (1, IDX_HEADS)), qc, kc, qm]
    rows = [va, vc, vb, pad_to(wi, BF16_ROWS)]
    return (jnp.concatenate(cols, axis=1).astype(BF16), jnp.concatenate(rows, axis=1).T.astype(BF16))


def _fox_consts():
    e = np.zeros((LANES, 1024), np.float32)
    oq = np.zeros((1, 512), np.float32)
    ok = np.zeros((1, 512), np.float32)
    for h in range(N_HEADS):
        base = h * LANES + HEAD_DIM
        for t in range(3):
            e[N_HEADS * t + h, base + t] = 1.0
            e[N_HEADS * t + h, 512 + base + 3 + t] = -1.0
            oq[0, base + 3 + t] = 1.0
            ok[0, base + t] = 1.0
    return (jnp.asarray(e, BF16), jnp.asarray(oq), jnp.asarray(ok))


def _rope_tables(positions, rot_dim, period):
    half = rot_dim // 2
    inv_freq = ROPE_THETA ** (-jnp.arange(0, rot_dim, 2, dtype=F32) / rot_dim)
    ang = positions.astype(F32).reshape(-1, 1) * inv_freq
    cos, sin = jnp.cos(ang), jnp.sin(ang)
    n = ang.shape[0]
    c = jnp.concatenate([cos, cos, jnp.ones((n, period - 2 * half), F32)], axis=1)
    s = jnp.concatenate([-sin, sin, jnp.zeros((n, period - 2 * half), F32)], axis=1)
    reps = LANES // period
    return jnp.tile(c, (1, reps)), jnp.tile(s, (1, reps))


def kernel(x, mem, positions, norm_mix_g, w_in, b_forget, diff_lambda, diff_subln_g, norm_mem_g,
           w_mem_kv, w_branch, w_gate, b_gate, w_out, norm_ffn_g, w_ffn_in, w_ffn_out, final_norm_g):
    batch, seq, d = x.shape
    depth = w_in.shape[0]
    n = batch * seq
    assert seq % TILE == 0, seq

    c16, s16 = _rope_tables(positions, HEAD_DIM // 4, HEAD_DIM)
    c8, s8 = _rope_tables(positions, DIFF_QK_DIM // 4, DIFF_QK_DIM)
    consts = _fox_consts()

    x2 = x.reshape(n, d)
    for l in range(depth):
        lam_init = 0.8 - 0.6 * math.exp(-0.3 * l)
        bf = jnp.pad(jnp.tile(b_forget[l], 3), (0, LANES - 3 * b_forget.shape[1])).reshape(1, LANES)
        w_row, w_t = _pack_w_in(w_in[l])
        z = _proj(x2, norm_mix_g[l].reshape(1, d), w_row, w_t, bf, (c16, s16, c8, s8), consts, seq=seq)
        o_a = _fox(z["qa"], z["ka"], z["vaT"], batch=batch, seq=seq)
        o_b = _dsa(z["qi"], z["wiT"], z["qb"], z["ki"], z["kb"], z["vbT"], batch=batch, seq=seq)
        o_c = _diff(z["qc"], z["kc"], z["vcT"], diff_lambda[l], jnp.full((1, 1), lam_init, F32),
                    jnp.tile(diff_subln_g[l], 2).reshape(1, LANES), batch=batch, seq=seq)
        w_kv = w_mem_kv[l]
        o_m = _mem_attn(z["qm"], mem, norm_mem_g[l].reshape(1, d), w_kv[:, :256].astype(BF16),
                        w_kv[:, 256:].T.astype(BF16), batch=batch, seq=seq)
        x2 = _merge(x2, norm_mix_g[l].reshape(1, d), o_a, o_b, o_c, o_m,
                    w_gate[l].astype(BF16), b_gate[l].reshape(1, -1), w_branch[l].astype(BF16),
                    w_out[l].astype(BF16))
        x2 = _ffn(x2, norm_ffn_g[l].reshape(1, d), w_ffn_in[l].astype(BF16), w_ffn_out[l].astype(BF16),
                  final_norm_g.reshape(1, d), final=(l == depth - 1))
    return x2.reshape(batch, seq, d)
```

```python
import functools
import math

import numpy as np
import jax
import jax.numpy as jnp
from jax import lax
from jax.experimental import pallas as pl
from jax.experimental.pallas import tpu as pltpu

F32 = jnp.float32
BF16 = jnp.bfloat16
I32 = jnp.int32

HEAD_DIM = 64
N_BRANCH = 4
N_HEADS = 4
IDX_HEADS = 8
IDX_DIM = 32
IDX_TOPK_MAX = 256
DIFF_QK_DIM = 32
ROPE_THETA = 500000.0
NORM_EPS = 1e-6
LANES = 128
BF16_ROWS = 16
NEG = -1e30
LOG2E = math.log2(math.e)
INT_MIN = -2 ** 31
TILE = 256
DENSE_ROWS = 512
VT_ROWS = HEAD_DIM + BF16_ROWS
KEY_BITS = 32
KEY_PAD = 1

VMEM_LIMIT = 56 * 1024 * 1024

_SEG = {}
_off = 0
for _name, _w in (("qa", 512), ("ka", 512), ("fa", 128), ("qb", 256), ("kb", 128),
                  ("qi", 256), ("ki", 256), ("qc", 256), ("kc", 256), ("qm", 256)):
    _SEG[_name] = (_off, _off + _w)
    _off += _w
PROJ_COLS = _off
_TSEG = {"va": (0, 256), "vc": (256, 512), "vb": (512, 576), "wi": (576, 592)}
PROJ_T_ROWS = 592

_DN_T = (((1,), (1,)), ((), ()))


def _dot(a, b):
    return jnp.dot(a, b, preferred_element_type=F32)


def _dot_t(a, b):
    return lax.dot_general(a, b, _DN_T, preferred_element_type=F32)


def _rms(x, g):
    return x * lax.rsqrt(jnp.mean(x * x, axis=-1, keepdims=True) + NORM_EPS) * g


def _cparams(sem):
    return pltpu.CompilerParams(dimension_semantics=sem, vmem_limit_bytes=VMEM_LIMIT)


def _split3(x):
    hi = x.astype(BF16)
    r = x - hi.astype(F32)
    mid = r.astype(BF16)
    lo = (r - mid.astype(F32)).astype(BF16)
    return hi, mid, lo


def _ones_rows(cols):
    r = lax.broadcasted_iota(I32, (BF16_ROWS, cols), 0)
    return jnp.where(r == 0, 1.0, 0.0).astype(BF16)


def _rope(z, c, s, half, period):
    lane = lax.broadcasted_iota(I32, z.shape, 1)
    up = pltpu.roll(z, LANES - half, 1)
    dn = pltpu.roll(z, half, 1)
    partner = jnp.where((lane & (period - 1)) < half, up, dn)
    return z * c + partner * s


def _proj_kernel(x_ref, g_ref, w_ref, wt_ref, bf_ref, c16_ref, s16_ref, c8_ref, s8_ref, e_ref,
                 oq_ref, ok_ref,
                 qa_o, ka_o, qb_o, kb_o, qi_o, ki_o, qc_o, kc_o, qm_o, vat_o, vct_o, vbt_o, wit_o,
                 carry_ref, *, tiles_per_batch):
    i = pl.program_id(0)
    tm = x_ref.shape[0]
    h = _rms(x_ref[...], g_ref[...]).astype(BF16)

    def mm(name):
        lo, hi = _SEG[name]
        return _dot(h, w_ref[:, lo:hi])

    f = mm("fa") + bf_ref[...]
    ls = jnp.minimum(f, 0.0) - jnp.log(1.0 + jnp.exp(-jnp.abs(f)))
    r_i = lax.broadcasted_iota(I32, (tm, tm), 0)
    c_i = lax.broadcasted_iota(I32, (tm, tm), 1)
    ltri = jnp.where(c_i <= r_i, 1.0, 0.0).astype(BF16)
    p_hi, p_mid, p_lo = _split3(ls)
    pre = _dot(ltri, p_hi) + _dot(ltri, p_mid) + _dot(ltri, p_lo)

    @pl.when(i % tiles_per_batch == 0)
    def _():
        carry_ref[...] = jnp.zeros_like(carry_ref)

    cum = pre + carry_ref[...]
    carry_ref[...] = cum[tm - 1:tm, :]
    c_hi, c_mid, c_lo = _split3(cum * LOG2E)
    lane = lax.broadcasted_iota(I32, cum.shape, 1)
    c_terms = jnp.where(lane < N_HEADS, c_hi, jnp.where(lane < 2 * N_HEADS, c_mid, c_lo))

    placed = _dot(c_terms, e_ref[...])
    qa = mm("qa") * (HEAD_DIM ** -0.5 * LOG2E) + placed[:, :512] + oq_ref[...]
    qa_o[...] = qa.astype(BF16)
    ka = mm("ka") + placed[:, 512:] + ok_ref[...]
    ka_o[...] = ka.astype(BF16)

    c16 = c16_ref[...]
    s16 = s16_ref[...]
    c8 = c8_ref[...]
    s8 = s8_ref[...]

    def roped(name, out_ref, c, s, half, period, scale):
        z = mm(name)
        for k in range(z.shape[1] // LANES):
            r = _rope(z[:, k * LANES:(k + 1) * LANES], c, s, half, period)
            if scale != 1.0:
                r = r * scale
            out_ref[:, k * LANES:(k + 1) * LANES] = r.astype(BF16)

    roped("qb", qb_o, c16, s16, 8, 64, HEAD_DIM ** -0.5 * LOG2E)
    roped("kb", kb_o, c16, s16, 8, 64, 1.0)
    roped("qi", qi_o, c8, s8, 4, 32, 1.0)
    roped("ki", ki_o, c8, s8, 4, 32, 1.0)
    roped("qc", qc_o, c8, s8, 4, 32, DIFF_QK_DIM ** -0.5 * LOG2E)
    roped("kc", kc_o, c8, s8, 4, 32, 1.0)
    qm_o[...] = (mm("qm") * (HEAD_DIM ** -0.5 * LOG2E)).astype(BF16)

    zt = _dot_t(wt_ref[...], h)
    ones = _ones_rows(tm)
    for name, out_ref, heads in (("va", vat_o, N_HEADS), ("vc", vct_o, N_HEADS), ("vb", vbt_o, 1)):
        lo = _TSEG[name][0]
        for hd in range(heads):
            out_ref[0, hd * VT_ROWS:hd * VT_ROWS + HEAD_DIM, :] = (
                zt[lo + hd * HEAD_DIM:lo + (hd + 1) * HEAD_DIM, :].astype(BF16))
            out_ref[0, hd * VT_ROWS + HEAD_DIM:(hd + 1) * VT_ROWS, :] = ones
    lo, hi = _TSEG["wi"]
    wit_o[...] = zt[lo:hi, :] * (1.0 / 16.0)


def _proj(x2, g, w, wt, bf, tabs, consts, *, seq):
    n, d = x2.shape
    tm = TILE
    grid = (n // tm,)
    row = lambda i: (i, 0)
    fixed2 = lambda i: (0, 0)
    chunk3 = lambda i: (i, 0, 0)
    e_place, oq, ok = consts
    names = ("qa", "ka", "qb", "kb", "qi", "ki", "qc", "kc", "qm")
    out_specs = [pl.BlockSpec((tm, _SEG[nm][1] - _SEG[nm][0]), row) for nm in names]
    out_shape = [jax.ShapeDtypeStruct((n, _SEG[nm][1] - _SEG[nm][0]), BF16) for nm in names]
    for heads in (N_HEADS, N_HEADS, 1):
        out_specs.append(pl.BlockSpec((1, heads * VT_ROWS, tm), chunk3))
        out_shape.append(jax.ShapeDtypeStruct((n // tm, heads * VT_ROWS, tm), BF16))
    out_specs.append(pl.BlockSpec((BF16_ROWS, tm), lambda i: (0, i)))
    out_shape.append(jax.ShapeDtypeStruct((BF16_ROWS, n), F32))
    outs = pl.pallas_call(
        functools.partial(_proj_kernel, tiles_per_batch=seq // tm),
        grid=grid,
        in_specs=[pl.BlockSpec((tm, d), row), pl.BlockSpec((1, d), fixed2),
                  pl.BlockSpec((d, PROJ_COLS), fixed2), pl.BlockSpec((PROJ_T_ROWS, d), fixed2),
                  pl.BlockSpec((1, LANES), fixed2),
                  pl.BlockSpec((tm, LANES), row), pl.BlockSpec((tm, LANES), row),
                  pl.BlockSpec((tm, LANES), row), pl.BlockSpec((tm, LANES), row),
                  pl.BlockSpec((LANES, 1024), fixed2),
                  pl.BlockSpec((1, 512), fixed2), pl.BlockSpec((1, 512), fixed2)],
        out_specs=out_specs,
        out_shape=out_shape,
        scratch_shapes=[pltpu.VMEM((1, LANES), F32)],
        compiler_params=_cparams(("arbitrary",)),
        name="proj",
    )(x2, g, w, wt, bf, *tabs, e_place, oq, ok)
    return dict(zip(names + ("vaT", "vcT", "vbT", "wiT"), outs))


def _softmax_init(m_ref, acc_ref):
    m_ref[...] = jnp.full(m_ref.shape, NEG, F32)
    acc_ref[...] = jnp.zeros(acc_ref.shape, F32)


def _softmax_steps(streams, m_ref, acc_ref):
    n = len(streams)
    sts, alphas, ps = [None] * n, [None] * n, [None] * n

    def scores(k):
        sts[k] = streams[k][1]()

    def probs(k):
        idx = streams[k][0]
        m_prev = m_ref[idx]
        m_new = m_prev
        for st in sts[k]:
            m_new = jnp.maximum(m_new, jnp.max(st, axis=0, keepdims=True))
        alphas[k] = jnp.exp2(m_prev - m_new)
        ps[k] = [jnp.exp2(st - m_new).astype(BF16) for st in sts[k]]
        m_ref[idx] = m_new

    def values(k):
        idx, _, vts = streams[k]
        pv = _dot(vts[0], ps[k][0])
        for vt, p in zip(vts[1:], ps[k][1:]):
            pv = pv + _dot(vt, p)
        acc_ref[idx] = alphas[k] * acc_ref[idx] + pv

    for stage in (scores, probs, values):
        for k in range(n):
            stage(k)


def _chunk_pairs(n_full, step):
    def body(jj, c):
        step([2 * jj, 2 * jj + 1])
        return c

    lax.fori_loop(0, n_full // 2, body, 0)

    @pl.when(n_full % 2 == 1)
    def _():
        step([n_full - 1])


def _pipelined_steps(n_steps, score_tiles, value_tiles, idxs, bufs, m_ref, acc_ref):
    def produce(n, st_buf, mx_buf):
        tiles = score_tiles(n)
        for k, sts in enumerate(tiles):
            mx = None
            for c, st in enumerate(sts):
                st_buf[k * len(sts) + c] = st
                cm = jnp.max(st, axis=0, keepdims=True)
                mx = cm if mx is None else jnp.maximum(mx, cm)
            mx_buf[k] = mx

    def consume(n, st_buf, mx_buf):
        vts = value_tiles(n)
        n_c = len(vts[0])
        alphas, ps = [], []
        for k, idx in enumerate(idxs):
            m_prev = m_ref[idx]
            m_new = jnp.maximum(m_prev, mx_buf[k])
            alphas.append(jnp.exp2(m_prev - m_new))
            ps.append([jnp.exp2(st_buf[k * n_c + c] - m_new).astype(BF16) for c in range(n_c)])
            m_ref[idx] = m_new
        for k, idx in enumerate(idxs):
            pv = _dot(vts[k][0], ps[k][0])
            for c in range(1, n_c):
                pv = pv + _dot(vts[k][c], ps[k][c])
            acc_ref[idx] = alphas[k] * acc_ref[idx] + pv

    (st_a, mx_a), (st_b, mx_b) = bufs

    @pl.when(n_steps > 0)
    def _():
        produce(0, st_a, mx_a)

    def body(s, c):
        produce(2 * s + 1, st_b, mx_b)
        consume(2 * s, st_a, mx_a)
        produce(2 * s + 2, st_a, mx_a)
        consume(2 * s + 1, st_b, mx_b)
        return c

    lax.fori_loop(0, n_steps // 2, body, 0)

    @pl.when(n_steps % 2 == 1)
    def _():
        consume(n_steps - 1, st_a, mx_a)


def _normalized(acc_ref, idx):
    acc = acc_ref[idx]
    return acc[0:HEAD_DIM, :] / acc[HEAD_DIM:HEAD_DIM + 1, :]


def _store_pair(o_ref, g, o0, o1, scale_row=None):
    o = jnp.concatenate([o0, o1], axis=0).T
    if scale_row is not None:
        o = o * scale_row
    o_ref[:, g * LANES:(g + 1) * LANES] = o.astype(o_ref.dtype)


def _diag_mask(t):
    key = lax.broadcasted_iota(I32, (t, t), 0)
    qry = lax.broadcasted_iota(I32, (t, t), 1)
    return key <= qry


def _masked_heads(q_ref, qm_ref, width):
    tq = q_ref.shape[0]
    lane = lax.broadcasted_iota(I32, (tq, LANES), 1)
    per_group = LANES // width
    for idx in range(2 * per_group):
        g = idx // per_group
        qg = q_ref[:, g * LANES:(g + 1) * LANES]
        lo = (idx % per_group) * width
        sel = (lane >= lo) & (lane < lo + width)
        qm_ref[idx] = jnp.where(sel, qg, jnp.zeros_like(qg))


def _fox_kernel(q_ref, k_ref, vt_ref, o_ref, m_ref, acc_ref, sta_ref, mxa_ref, stb_ref, mxb_ref):
    i = pl.program_id(1)
    t = q_ref.shape[0]
    nq = vt_ref.shape[0]
    _softmax_init(m_ref, acc_ref)

    def chunks_of(n):
        return [jnp.minimum(j, nq - 1) for j in (2 * n, 2 * n + 1)]

    def score_tiles(n):
        return [[_dot_t(k_ref[pl.ds(pl.multiple_of(j * t, t), t), h * LANES:(h + 1) * LANES],
                        q_ref[:, h * LANES:(h + 1) * LANES]) for j in chunks_of(n)] for h in range(N_HEADS)]

    def value_tiles(n):
        return [[vt_ref[j, h * VT_ROWS:(h + 1) * VT_ROWS, :] for j in chunks_of(n)] for h in range(N_HEADS)]

    _pipelined_steps(i // 2, score_tiles, value_tiles, list(range(N_HEADS)),
                     ((sta_ref, mxa_ref), (stb_ref, mxb_ref)), m_ref, acc_ref)

    def step(js, masked=False):
        streams = []
        for h in range(N_HEADS):
            def scores(h=h):
                q = q_ref[:, h * LANES:(h + 1) * LANES]
                sts = [_dot_t(k_ref[pl.ds(pl.multiple_of(j * t, t), t), h * LANES:(h + 1) * LANES], q)
                       for j in js]
                if masked:
                    sts = [jnp.where(_diag_mask(t), st, NEG) for st in sts]
                return sts
            streams.append((h, scores, [vt_ref[j, h * VT_ROWS:(h + 1) * VT_ROWS, :] for j in js]))
        _softmax_steps(streams, m_ref, acc_ref)

    @pl.when(i % 2 == 1)
    def _():
        step([i - 1])

    step([i], masked=True)
    for g in range(N_HEADS // 2):
        _store_pair(o_ref, g, _normalized(acc_ref, 2 * g), _normalized(acc_ref, 2 * g + 1))


def _fox(qa, ka, vat, *, batch, seq):
    n = batch * seq
    t = TILE
    nq = seq // t
    return pl.pallas_call(
        _fox_kernel,
        grid=(batch, nq),
        in_specs=[pl.BlockSpec((t, 512), lambda b, i: (b * nq + i, 0)),
                  pl.BlockSpec((seq, 512), lambda b, i: (b, 0)),
                  pl.BlockSpec((nq, N_HEADS * VT_ROWS, t), lambda b, i: (b, 0, 0))],
        out_specs=pl.BlockSpec((t, 256), lambda b, i: (b * nq + i, 0)),
        out_shape=jax.ShapeDtypeStruct((n, 256), BF16),
        scratch_shapes=[pltpu.VMEM((N_HEADS, 1, t), F32), pltpu.VMEM((N_HEADS, VT_ROWS, t), F32),
                        pltpu.VMEM((2 * N_HEADS, t, t), F32), pltpu.VMEM((N_HEADS, 1, t), F32),
                        pltpu.VMEM((2 * N_HEADS, t, t), F32), pltpu.VMEM((N_HEADS, 1, t), F32)],
        compiler_params=_cparams(("arbitrary", "arbitrary")),
        name="fox",
    )(qa, ka, vat)


def _diff_kernel(q_ref, k_ref, vt_ref, dl_ref, li_ref, g_ref, o_ref, qm_ref, m_ref, acc_ref,
                 sta_ref, mxa_ref, stb_ref, mxb_ref):
    i = pl.program_id(1)
    t = q_ref.shape[0]
    nq = vt_ref.shape[0]
    n_streams = 2 * N_HEADS
    _softmax_init(m_ref, acc_ref)
    _masked_heads(q_ref, qm_ref, DIFF_QK_DIM)

    def chunks_of(n):
        return [jnp.minimum(j, nq - 1) for j in (2 * n, 2 * n + 1)]

    def score_tiles(n):
        tiles = []
        for g in range(N_HEADS // 2):
            ks = [k_ref[pl.ds(pl.multiple_of(j * t, t), t), g * LANES:(g + 1) * LANES] for j in chunks_of(n)]
            tiles += [[_dot_t(k, qm_ref[idx]) for k in ks] for idx in range(4 * g, 4 * g + 4)]
        return tiles

    def value_tiles(n):
        return [[vt_ref[j, (idx // 2) * VT_ROWS:(idx // 2 + 1) * VT_ROWS, :] for j in chunks_of(n)]
                for idx in range(n_streams)]

    _pipelined_steps(i // 2, score_tiles, value_tiles, list(range(n_streams)),
                     ((sta_ref, mxa_ref), (stb_ref, mxb_ref)), m_ref, acc_ref)

    def step(js, masked=False):
        for g in range(N_HEADS // 2):
            ks = [k_ref[pl.ds(pl.multiple_of(j * t, t), t), g * LANES:(g + 1) * LANES] for j in js]
            streams = []
            for idx in range(4 * g, 4 * g + 4):
                def scores(idx=idx, ks=ks):
                    sts = [_dot_t(k, qm_ref[idx]) for k in ks]
                    if masked:
                        sts = [jnp.where(_diag_mask(t), st, NEG) for st in sts]
                    return sts
                h = idx // 2
                streams.append((idx, scores, [vt_ref[j, h * VT_ROWS:(h + 1) * VT_ROWS, :] for j in js]))
            _softmax_steps(streams, m_ref, acc_ref)

    @pl.when(i % 2 == 1)
    def _():
        step([i - 1])

    step([i], masked=True)

    dl = dl_ref[...]
    lam_init = li_ref[...]
    lam = (jnp.exp(jnp.sum(dl[0:1] * dl[1:2], axis=-1, keepdims=True))
           - jnp.exp(jnp.sum(dl[2:3] * dl[3:4], axis=-1, keepdims=True)) + lam_init)
    scale_row = g_ref[...] * (1.0 - lam_init)
    for g in range(N_HEADS // 2):
        outs = []
        for h in (2 * g, 2 * g + 1):
            o = _normalized(acc_ref, 2 * h) - lam * _normalized(acc_ref, 2 * h + 1)
            outs.append(o * lax.rsqrt(jnp.mean(o * o, axis=0, keepdims=True) + NORM_EPS))
        _store_pair(o_ref, g, outs[0], outs[1], scale_row)


def _diff(qc, kc, vct, dl, lam_init, g2, *, batch, seq):
    n = batch * seq
    t = TILE
    nq = seq // t
    fixed = lambda b, i: (0, 0)
    return pl.pallas_call(
        _diff_kernel,
        grid=(batch, nq),
        in_specs=[pl.BlockSpec((t, 256), lambda b, i: (b * nq + i, 0)),
                  pl.BlockSpec((seq, 256), lambda b, i: (b, 0)),
                  pl.BlockSpec((nq, N_HEADS * VT_ROWS, t), lambda b, i: (b, 0, 0)),
                  pl.BlockSpec((4, DIFF_QK_DIM), fixed), pl.BlockSpec((1, 1), fixed),
                  pl.BlockSpec((1, LANES), fixed)],
        out_specs=pl.BlockSpec((t, 256), lambda b, i: (b * nq + i, 0)),
        out_shape=jax.ShapeDtypeStruct((n, 256), BF16),
        scratch_shapes=[pltpu.VMEM((2 * N_HEADS, t, LANES), BF16),
                        pltpu.VMEM((2 * N_HEADS, 1, t), F32),
                        pltpu.VMEM((2 * N_HEADS, VT_ROWS, t), F32),
                        pltpu.VMEM((4 * N_HEADS, t, t), F32), pltpu.VMEM((2 * N_HEADS, 1, t), F32),
                        pltpu.VMEM((4 * N_HEADS, t, t), F32), pltpu.VMEM((2 * N_HEADS, 1, t), F32)],
        compiler_params=_cparams(("arbitrary", "arbitrary")),
        name="diff",
    )(qc, kc, vct, dl, lam_init, g2)


def _mem_kernel(q_ref, mem_ref, g_ref, wk_ref, wvt_ref, o_ref, k_s, vt_s, qm_ref):
    i = pl.program_id(1)
    m_len = mem_ref.shape[1]

    @pl.when(i == 0)
    def _():
        hm = _rms(mem_ref[0], g_ref[...]).astype(BF16)
        k_s[...] = _dot(hm, wk_ref[...]).astype(BF16)
        vt = _dot_t(wvt_ref[...], hm)
        ones = _ones_rows(m_len)
        for h in range(N_HEADS):
            vt_s[h, 0:HEAD_DIM, :] = vt[h * HEAD_DIM:(h + 1) * HEAD_DIM, :].astype(BF16)
            vt_s[h, HEAD_DIM:VT_ROWS, :] = ones

    _masked_heads(q_ref, qm_ref, HEAD_DIM)
    sts = [_dot_t(k_s[:, (h // 2) * LANES:(h // 2 + 1) * LANES], qm_ref[h]) for h in range(N_HEADS)]
    ps = [jnp.exp2(st - jnp.max(st, axis=0, keepdims=True)).astype(BF16) for st in sts]
    accs = [_dot(vt_s[h], ps[h]) for h in range(N_HEADS)]
    outs = [acc[0:HEAD_DIM, :] / acc[HEAD_DIM:HEAD_DIM + 1, :] for acc in accs]
    for g in range(N_HEADS // 2):
        _store_pair(o_ref, g, outs[2 * g], outs[2 * g + 1])


def _mem_attn(qm, mem, g, wk, wvt, *, batch, seq):
    n = batch * seq
    t = TILE
    nq = seq // t
    m_len, d = mem.shape[1], mem.shape[2]
    fixed = lambda b, i: (0, 0)
    return pl.pallas_call(
        _mem_kernel,
        grid=(batch, nq),
        in_specs=[pl.BlockSpec((t, 256), lambda b, i: (b * nq + i, 0)),
                  pl.BlockSpec((1, m_len, d), lambda b, i: (b, 0, 0)),
                  pl.BlockSpec((1, d), fixed), pl.BlockSpec((d, 256), fixed), pl.BlockSpec((256, d), fixed)],
        out_specs=pl.BlockSpec((t, 256), lambda b, i: (b * nq + i, 0)),
        out_shape=jax.ShapeDtypeStruct((n, 256), BF16),
        scratch_shapes=[pltpu.VMEM((m_len, 256), BF16), pltpu.VMEM((N_HEADS, VT_ROWS, m_len), BF16),
                        pltpu.VMEM((N_HEADS, t, LANES), BF16)],
        compiler_params=_cparams(("arbitrary", "arbitrary")),
        name="mem_attn",
    )(qm, mem, g, wk, wvt)


def _store_bit_planes(key, planes_ref, j):
    tk = key.shape[0]
    assert tk == 8 * KEY_BITS, tk
    x = key ^ jnp.int32(INT_MIN)
    a = [x[8 * g:8 * g + 8, :] for g in range(KEY_BITS)]
    dist, mask = KEY_BITS // 2, 0x0000FFFF
    while dist:
        k = 0
        while k < KEY_BITS:
            swap = (a[k] ^ lax.shift_right_logical(a[k + dist], jnp.int32(dist))) & jnp.int32(mask)
            a[k] = a[k] ^ swap
            a[k + dist] = a[k + dist] ^ (swap << dist)
            k = (k + dist + 1) & ~dist
        dist >>= 1
        mask ^= mask << dist
    for b in range(KEY_BITS):
        planes_ref[j, b] = a[b]
    planes_ref[j, KEY_BITS] = jnp.full(a[0].shape, -1, I32)


def _dsa_kernel(qi_ref, wit_ref, qb_ref, ki_ref, kb_ref, vbt_ref, o_ref,
                keys_ref, planes_ref, cand_ref, qim_ref, qbm_ref, m_ref, acc_ref,
                sta_ref, mxa_ref, stb_ref, mxb_ref, *, top_k, idx_bits):
    i = pl.program_id(1)
    t = qi_ref.shape[0]
    n_ch = i + 1

    lane_q = lax.broadcasted_iota(I32, (t, 2 * LANES), 1)
    qi = qi_ref[...]
    for h in range(IDX_HEADS):
        qim_ref[h] = jnp.where((lane_q >> 5) == h, qi, jnp.zeros_like(qi))
    wit = wit_ref[...]

    def score_step(js, masked=False):
        kcs = [ki_ref[pl.ds(pl.multiple_of(j * t, t), t), :] for j in js]
        scores = [jnp.zeros((t, t), F32) for _ in js]
        for h in range(IDX_HEADS):
            w_h = wit[h:h + 1, :]
            scores = [sc + jnp.maximum(_dot_t(kc, qim_ref[h]), 0.0) * w_h for sc, kc in zip(scores, kcs)]
        for j, score in zip(js, scores):
            bits = lax.bitcast_convert_type(score, I32)
            key = bits ^ ((bits >> 31) & jnp.int32(0x7FFFFFFF))
            key = jnp.where(key == -1, 0, key)
            if masked:
                key = jnp.where(_diag_mask(t), key, jnp.int32(INT_MIN))
            keys_ref[j] = key
            _store_bit_planes(key, planes_ref, j)

    _chunk_pairs(i, score_step)
    score_step([i], masked=True)
    for pad in range(KEY_PAD):
        keys_ref[n_ch + pad] = jnp.full((t, t), INT_MIN, I32)
        planes_ref[n_ch + pad, 0:KEY_BITS] = jnp.zeros((KEY_BITS, 8, t), I32)
        planes_ref[n_ch + pad, KEY_BITS] = jnp.full((8, t), -1, I32)
    n_pairs = (n_ch + 1) // 2

    def count(hits):
        def body(jj, acc):
            for j in (2 * jj, 2 * jj + 1):
                acc = acc + jnp.sum(hits(keys_ref[j], j).reshape(t // 8, 8, t), axis=0)
            return acc
        acc = lax.fori_loop(0, n_pairs, body, jnp.zeros((8, t), F32))
        return jnp.sum(acc, axis=0, keepdims=True)

    def bit_step(bi, carry):
        rem, n_cand, thr_bits, flip = carry
        prev = jnp.where(bi == 0, KEY_BITS, bi - 1)

        def body(jj, acc):
            for j in (2 * jj, 2 * jj + 1):
                cand = cand_ref[j] & (planes_ref[j, prev] ^ flip)
                cand_ref[j] = cand
                acc = acc + lax.population_count(cand & planes_ref[j, bi])
            return acc

        ones = lax.fori_loop(0, n_pairs, body, jnp.zeros((8, t), I32))
        n_ones = jnp.sum(ones.astype(F32), axis=0, keepdims=True)
        take = n_ones >= rem
        bit = jnp.int32(1) << (KEY_BITS - 1 - bi)
        return (jnp.where(take, rem, rem - n_ones), jnp.where(take, n_ones, n_cand - n_ones),
                jnp.where(take, thr_bits | bit, thr_bits), jnp.where(take, 0, -1))

    def reset_cand(j, c):
        cand_ref[j] = jnp.full((8, t), -1, I32)
        return c

    lax.fori_loop(0, 2 * n_pairs, reset_cand, 0)
    n_all = jnp.full((1, t), 2.0 * t, F32) * n_pairs.astype(F32)
    need, n_eq, thr_bits, _ = lax.fori_loop(
        0, KEY_BITS, bit_step, (jnp.full((1, t), float(top_k), F32), n_all,
                                jnp.zeros((1, t), I32), jnp.zeros((1, t), I32)))
    thr = thr_bits ^ jnp.int32(INT_MIN)

    tie = jnp.where(thr > INT_MIN, jnp.where(n_eq > need, 1.0, 0.0), 0.0)

    @pl.when(jnp.max(tie) > 0.0)
    def _():
        def key_pos(j):
            return j * t + lax.broadcasted_iota(I32, (t, t), 0)

        def idx_step(b, cut):
            cand = cut + (jnp.int32(1) << (idx_bits - 1 - b))
            cnt = count(lambda kc, j: jnp.where(kc == thr, jnp.where(key_pos(j) < cand, 1.0, 0.0), 0.0))
            return jnp.where(cnt < need, cand, cut)

        cut = lax.fori_loop(0, idx_bits, idx_step, jnp.zeros((1, t), I32))
        cut = jnp.where(tie > 0.0, cut, jnp.int32(2 ** 30))

        def demote(j, c):
            kc = keys_ref[j]
            demoted = jnp.where(key_pos(j) > cut, kc - 1, kc)
            keys_ref[j] = jnp.where(kc == thr, demoted, kc)
            return c

        lax.fori_loop(0, n_ch, demote, 0)

    thr_sel = jnp.maximum(thr, jnp.int32(INT_MIN + 1))

    _softmax_init(m_ref, acc_ref)
    _masked_heads(qb_ref, qbm_ref, HEAD_DIM)

    nq = vbt_ref.shape[0]

    def to_bias(j, c):
        bias = jnp.where(keys_ref[j] >= thr_sel, 0.0, NEG)
        keys_ref[j] = lax.bitcast_convert_type(bias, I32)
        return c

    lax.fori_loop(0, 2 * n_pairs, to_bias, 0)

    def chunks_of(n):
        return [(jnp.minimum(j, nq), jnp.minimum(j, nq - 1)) for j in (2 * n, 2 * n + 1)]

    def score_tiles(n):
        biases = [lax.bitcast_convert_type(keys_ref[jk], F32) for jk, _ in chunks_of(n)]
        ks = [kb_ref[pl.ds(pl.multiple_of(jv * t, t), t), :] for _, jv in chunks_of(n)]
        return [[_dot_t(k, qbm_ref[h]) + bias for bias, k in zip(biases, ks)] for h in range(N_HEADS)]

    def value_tiles(n):
        return [[vbt_ref[jv] for _, jv in chunks_of(n)]] * N_HEADS

    _pipelined_steps(n_pairs, score_tiles, value_tiles, list(range(N_HEADS)),
                     ((sta_ref, mxa_ref), (stb_ref, mxb_ref)), m_ref, acc_ref)
    for g in range(N_HEADS // 2):
        _store_pair(o_ref, g, _normalized(acc_ref, 2 * g), _normalized(acc_ref, 2 * g + 1))


def _dsa(qi, wit, qb, ki, kb, vbt, *, batch, seq):
    n = batch * seq
    t = TILE
    nq = seq // t
    top_k = min(IDX_TOPK_MAX, seq // 4)
    idx_bits = max(1, int(math.ceil(math.log2(seq))))
    qrow = lambda b, i: (b * nq + i, 0)
    kv = lambda b, i: (b, 0)
    return pl.pallas_call(
        functools.partial(_dsa_kernel, top_k=top_k, idx_bits=idx_bits),
        grid=(batch, nq),
        in_specs=[pl.BlockSpec((t, 256), qrow), pl.BlockSpec((BF16_ROWS, t), lambda b, i: (0, b * nq + i)),
                  pl.BlockSpec((t, 256), qrow),
                  pl.BlockSpec((seq, 256), kv), pl.BlockSpec((seq, LANES), kv),
                  pl.BlockSpec((nq, VT_ROWS, t), lambda b, i: (b, 0, 0))],
        out_specs=pl.BlockSpec((t, 256), qrow),
        out_shape=jax.ShapeDtypeStruct((n, 256), BF16),
        scratch_shapes=[pltpu.VMEM((nq + KEY_PAD, t, t), I32),
                        pltpu.VMEM((nq + KEY_PAD, KEY_BITS + 1, 8, t), I32),
                        pltpu.VMEM((nq + KEY_PAD, 8, t), I32),
                        pltpu.VMEM((IDX_HEADS, t, 2 * LANES), BF16),
                        pltpu.VMEM((N_HEADS, t, LANES), BF16),
                        pltpu.VMEM((N_HEADS, 1, t), F32), pltpu.VMEM((N_HEADS, VT_ROWS, t), F32),
                        pltpu.VMEM((2 * N_HEADS, t, t), F32), pltpu.VMEM((N_HEADS, 1, t), F32),
                        pltpu.VMEM((2 * N_HEADS, t, t), F32), pltpu.VMEM((N_HEADS, 1, t), F32)],
        compiler_params=_cparams(("arbitrary", "arbitrary")),
        name="dsa",
    )(qi, wit, qb, ki, kb, vbt)


def _merge_kernel(x_ref, g_ref, oa_ref, ob_ref, oc_ref, om_ref, wg_ref, bg_ref, wb_ref, wo_ref, y_ref):
    x = x_ref[...]
    d = x.shape[1]
    h = _rms(x, g_ref[...]).astype(BF16)
    acc = jnp.zeros(x.shape, F32)
    for nb, o_ref in enumerate((oa_ref, ob_ref, oc_ref, om_ref)):
        gate = jax.nn.sigmoid(_dot(h, wg_ref[:, nb * d:(nb + 1) * d]) + bg_ref[:, nb * d:(nb + 1) * d])
        acc = acc + gate * _dot(o_ref[...], wb_ref[nb])
    y_ref[...] = x + _dot(acc.astype(BF16), wo_ref[...])


def _merge(x2, g, oa, ob, oc, om, wg, bg, wb, wo):
    n, d = x2.shape
    tm = min(DENSE_ROWS, n)
    row = lambda i: (i, 0)
    fixed2 = lambda i: (0, 0)
    bw = oa.shape[1]
    once = pl.Buffered(1)
    return pl.pallas_call(
        _merge_kernel,
        grid=(n // tm,),
        in_specs=[pl.BlockSpec((tm, d), row), pl.BlockSpec((1, d), fixed2),
                  pl.BlockSpec((tm, bw), row), pl.BlockSpec((tm, bw), row),
                  pl.BlockSpec((tm, bw), row), pl.BlockSpec((tm, bw), row),
                  pl.BlockSpec((d, N_BRANCH * d), fixed2, pipeline_mode=once),
                  pl.BlockSpec((1, N_BRANCH * d), fixed2),
                  pl.BlockSpec((N_BRANCH, bw, d), lambda i: (0, 0, 0), pipeline_mode=once),
                  pl.BlockSpec((d, d), fixed2, pipeline_mode=once)],
        out_specs=pl.BlockSpec((tm, d), row),
        out_shape=jax.ShapeDtypeStruct((n, d), F32),
        compiler_params=_cparams(("arbitrary",)),
        name="merge",
    )(x2, g, oa, ob, oc, om, wg, bg, wb, wo)


def _ffn_kernel(x_ref, g_ref, wi_ref, wo_ref, gf_ref, y_ref, *, hidden, chunk, final):
    x = x_ref[...]
    h = _rms(x, g_ref[...]).astype(BF16)
    acc = jnp.zeros(x.shape, F32)
    for c in range(hidden // chunk):
        a = _dot(h, wi_ref[:, c * chunk:(c + 1) * chunk])
        b = _dot(h, wi_ref[:, hidden + c * chunk:hidden + (c + 1) * chunk])
        t = (a * jax.nn.sigmoid(a)) * b
        acc = acc + _dot(t.astype(BF16), wo_ref[c * chunk:(c + 1) * chunk, :])
    y = x + acc
    if final:
        y = _rms(y, gf_ref[...])
    y_ref[...] = y


def _ffn(x2, g, w_in, w_out, g_final, *, final):
    n, d = x2.shape
    tm = min(DENSE_ROWS, n)
    hidden = w_out.shape[0]
    chunk = hidden // 4 if (hidden // 4) % LANES == 0 else hidden
    row = lambda i: (i, 0)
    fixed2 = lambda i: (0, 0)
    once = pl.Buffered(1)
    return pl.pallas_call(
        functools.partial(_ffn_kernel, hidden=hidden, chunk=chunk, final=final),
        grid=(n // tm,),
        in_specs=[pl.BlockSpec((tm, d), row), pl.BlockSpec((1, d), fixed2),
                  pl.BlockSpec((d, 2 * hidden), fixed2, pipeline_mode=once),
                  pl.BlockSpec((hidden, d), fixed2, pipeline_mode=once),
                  pl.BlockSpec((1, d), fixed2)],
        out_specs=pl.BlockSpec((tm, d), row),
        out_shape=jax.ShapeDtypeStruct((n, d), F32),
        compiler_params=_cparams(("arbitrary",)),
        name="ffn",
    )(x2, g, w_in, w_out, g_final)


_IN_SPLITS = (256, 256, 256, 4, 256, 64, 64, 256, 32, 8, 256, 256, 256, 256)


def _pack_w_in(w):
    d = w.shape[0]
    offs = np.cumsum((0,) + _IN_SPLITS)
    qa, ka, va, fa, qb, kb, vb, qi, ki, wi, qc, kc, vc, qm = (
        w[:, offs[k]:offs[k + 1]] for k in range(len(_IN_SPLITS)))

    def pad_heads(a):
        a = a.reshape(d, N_HEADS, HEAD_DIM)
        return jnp.pad(a, ((0, 0), (0, 0), (0, LANES - HEAD_DIM))).reshape(d, N_HEADS * LANES)

    def pad_to(a, wd):
        return jnp.pad(a, ((0, 0), (0, wd - a.shape[1])))

    cols = [pad_heads(qa), pad_heads(ka), pad_to(jnp.tile(fa, (1, 3)), LANES), qb,
            jnp.concatenate([kb, kb], axis=1),
            qi, jnp.tile(ki, (1, IDX_HEADS)), qc, kc, qm]
    rows = [va, vc, vb, pad_to(wi, BF16_ROWS)]
    return (jnp.concatenate(cols, axis=1).astype(BF16), jnp.concatenate(rows, axis=1).T.astype(BF16))


def _fox_consts():
    e = np.zeros((LANES, 1024), np.float32)
    oq = np.zeros((1, 512), np.float32)
    ok = np.zeros((1, 512), np.float32)
    for h in range(N_HEADS):
        base = h * LANES + HEAD_DIM
        for t in range(3):
            e[N_HEADS * t + h, base + t] = 1.0
            e[N_HEADS * t + h, 512 + base + 3 + t] = -1.0
            oq[0, base + 3 + t] = 1.0
            ok[0, base + t] = 1.0
    return (jnp.asarray(e, BF16), jnp.asarray(oq), jnp.asarray(ok))


def _rope_tables(positions, rot_dim, period):
    half = rot_dim // 2
    inv_freq = ROPE_THETA ** (-jnp.arange(0, rot_dim, 2, dtype=F32) / rot_dim)
    ang = positions.astype(F32).reshape(-1, 1) * inv_freq
    cos, sin = jnp.cos(ang), jnp.sin(ang)
    n = ang.shape[0]
    c = jnp.concatenate([cos, cos, jnp.ones((n, period - 2 * half), F32)], axis=1)
    s = jnp.concatenate([-sin, sin, jnp.zeros((n, period - 2 * half), F32)], axis=1)
    reps = LANES // period
    return jnp.tile(c, (1, reps)), jnp.tile(s, (1, reps))


def kernel(x, mem, positions, norm_mix_g, w_in, b_forget, diff_lambda, diff_subln_g, norm_mem_g,
           w_mem_kv, w_branch, w_gate, b_gate, w_out, norm_ffn_g, w_ffn_in, w_ffn_out, final_norm_g):
    batch, seq, d = x.shape
    depth = w_in.shape[0]
    n = batch * seq
    assert seq % TILE == 0, seq

    c16, s16 = _rope_tables(positions, HEAD_DIM // 4, HEAD_DIM)
    c8, s8 = _rope_tables(positions, DIFF_QK_DIM // 4, DIFF_QK_DIM)
    consts = _fox_consts()

    x2 = x.reshape(n, d)
    for l in range(depth):
        lam_init = 0.8 - 0.6 * math.exp(-0.3 * l)
        bf = jnp.pad(jnp.tile(b_forget[l], 3), (0, LANES - 3 * b_forget.shape[1])).reshape(1, LANES)
        w_row, w_t = _pack_w_in(w_in[l])
        z = _proj(x2, norm_mix_g[l].reshape(1, d), w_row, w_t, bf, (c16, s16, c8, s8), consts, seq=seq)
        o_a = _fox(z["qa"], z["ka"], z["vaT"], batch=batch, seq=seq)
        o_b = _dsa(z["qi"], z["wiT"], z["qb"], z["ki"], z["kb"], z["vbT"], batch=batch, seq=seq)
        o_c = _diff(z["qc"], z["kc"], z["vcT"], diff_lambda[l], jnp.full((1, 1), lam_init, F32),
                    jnp.tile(diff_subln_g[l], 2).reshape(1, LANES), batch=batch, seq=seq)
        w_kv = w_mem_kv[l]
        o_m = _mem_attn(z["qm"], mem, norm_mem_g[l].reshape(1, d), w_kv[:, :256].astype(BF16),
                        w_kv[:, 256:].T.astype(BF16), batch=batch, seq=seq)
        x2 = _merge(x2, norm_mix_g[l].reshape(1, d), o_a, o_b, o_c, o_m,
                    w_gate[l].astype(BF16), b_gate[l].reshape(1, -1), w_branch[l].astype(BF16),
                    w_out[l].astype(BF16))
        x2 = _ffn(x2, norm_ffn_g[l].reshape(1, d), w_ffn_in[l].astype(BF16), w_ffn_out[l].astype(BF16),
                  final_norm_g.reshape(1, d), final=(l == depth - 1))
    return x2.reshape(batch, seq, d)
```

```python
import functools
import math

import numpy as np
import jax
import jax.numpy as jnp
from jax import lax
from jax.experimental import pallas as pl
from jax.experimental.pallas import tpu as pltpu

F32 = jnp.float32
BF16 = jnp.bfloat16
I32 = jnp.int32

HEAD_DIM = 64
N_BRANCH = 4
N_HEADS = 4
IDX_HEADS = 8
IDX_DIM = 32
IDX_TOPK_MAX = 256
DIFF_QK_DIM = 32
ROPE_THETA = 500000.0
NORM_EPS = 1e-6
LANES = 128
BF16_ROWS = 16
NEG = -1e30
LOG2E = math.log2(math.e)
INT_MIN = -2 ** 31
TILE = 256
DENSE_ROWS = 512
VT_ROWS = HEAD_DIM + BF16_ROWS
KEY_BITS = 32
KEY_PAD = 1

VMEM_LIMIT = 56 * 1024 * 1024

_SEG = {}
_off = 0
for _name, _w in (("qa", 512), ("ka", 512), ("fa", 128), ("qb", 256), ("kb", 128),
                  ("qi", 256), ("ki", 256), ("qc", 256), ("kc", 256), ("qm", 256)):
    _SEG[_name] = (_off, _off + _w)
    _off += _w
PROJ_COLS = _off
_TSEG = {"va": (0, 256), "vc": (256, 512), "vb": (512, 576), "wi": (576, 592)}
PROJ_T_ROWS = 592

_DN_T = (((1,), (1,)), ((), ()))


def _dot(a, b):
    return jnp.dot(a, b, preferred_element_type=F32)


def _dot_t(a, b):
    return lax.dot_general(a, b, _DN_T, preferred_element_type=F32)


def _rms(x, g):
    return x * lax.rsqrt(jnp.mean(x * x, axis=-1, keepdims=True) + NORM_EPS) * g


def _cparams(sem):
    return pltpu.CompilerParams(dimension_semantics=sem, vmem_limit_bytes=VMEM_LIMIT)


def _split3(x):
    hi = x.astype(BF16)
    r = x - hi.astype(F32)
    mid = r.astype(BF16)
    lo = (r - mid.astype(F32)).astype(BF16)
    return hi, mid, lo


def _ones_rows(cols):
    r = lax.broadcasted_iota(I32, (BF16_ROWS, cols), 0)
    return jnp.where(r == 0, 1.0, 0.0).astype(BF16)


def _rope(z, c, s, half, period):
    lane = lax.broadcasted_iota(I32, z.shape, 1)
    up = pltpu.roll(z, LANES - half, 1)
    dn = pltpu.roll(z, half, 1)
    partner = jnp.where((lane & (period - 1)) < half, up, dn)
    return z * c + partner * s


def _proj_kernel(x_ref, g_ref, w_ref, wt_ref, bf_ref, c16_ref, s16_ref, c8_ref, s8_ref, e_ref,
                 oq_ref, ok_ref,
                 qa_o, ka_o, qb_o, kb_o, qi_o, ki_o, qc_o, kc_o, qm_o, vat_o, vct_o, vbt_o, wit_o,
                 carry_ref, *, tiles_per_batch):
    i = pl.program_id(0)
    tm = x_ref.shape[0]
    h = _rms(x_ref[...], g_ref[...]).astype(BF16)

    def mm(name):
        lo, hi = _SEG[name]
        return _dot(h, w_ref[:, lo:hi])

    f = mm("fa") + bf_ref[...]
    ls = jnp.minimum(f, 0.0) - jnp.log(1.0 + jnp.exp(-jnp.abs(f)))
    r_i = lax.broadcasted_iota(I32, (tm, tm), 0)
    c_i = lax.broadcasted_iota(I32, (tm, tm), 1)
    ltri = jnp.where(c_i <= r_i, 1.0, 0.0).astype(BF16)
    p_hi, p_mid, p_lo = _split3(ls)
    pre = _dot(ltri, p_hi) + _dot(ltri, p_mid) + _dot(ltri, p_lo)

    @pl.when(i % tiles_per_batch == 0)
    def _():
        carry_ref[...] = jnp.zeros_like(carry_ref)

    cum = pre + carry_ref[...]
    carry_ref[...] = cum[tm - 1:tm, :]
    c_hi, c_mid, c_lo = _split3(cum * LOG2E)
    lane = lax.broadcasted_iota(I32, cum.shape, 1)
    c_terms = jnp.where(lane < N_HEADS, c_hi, jnp.where(lane < 2 * N_HEADS, c_mid, c_lo))

    placed = _dot(c_terms, e_ref[...])
    qa = mm("qa") * (HEAD_DIM ** -0.5 * LOG2E) + placed[:, :512] + oq_ref[...]
    qa_o[...] = qa.astype(BF16)
    ka = mm("ka") + placed[:, 512:] + ok_ref[...]
    ka_o[...] = ka.astype(BF16)

    c16 = c16_ref[...]
    s16 = s16_ref[...]
    c8 = c8_ref[...]
    s8 = s8_ref[...]

    def roped(name, out_ref, c, s, half, period, scale):
        z = mm(name)
        for k in range(z.shape[1] // LANES):
            r = _rope(z[:, k * LANES:(k + 1) * LANES], c, s, half, period)
            if scale != 1.0:
                r = r * scale
            out_ref[:, k * LANES:(k + 1) * LANES] = r.astype(BF16)

    roped("qb", qb_o, c16, s16, 8, 64, HEAD_DIM ** -0.5 * LOG2E)
    roped("kb", kb_o, c16, s16, 8, 64, 1.0)
    roped("qi", qi_o, c8, s8, 4, 32, 1.0)
    roped("ki", ki_o, c8, s8, 4, 32, 1.0)
    roped("qc", qc_o, c8, s8, 4, 32, DIFF_QK_DIM ** -0.5 * LOG2E)
    roped("kc", kc_o, c8, s8, 4, 32, 1.0)
    qm_o[...] = (mm("qm") * (HEAD_DIM ** -0.5 * LOG2E)).astype(BF16)

    zt = _dot_t(wt_ref[...], h)
    ones = _ones_rows(tm)
    for name, out_ref, heads in (("va", vat_o, N_HEADS), ("vc", vct_o, N_HEADS), ("vb", vbt_o, 1)):
        lo = _TSEG[name][0]
        for hd in range(heads):
            out_ref[0, hd * VT_ROWS:hd * VT_ROWS + HEAD_DIM, :] = (
                zt[lo + hd * HEAD_DIM:lo + (hd + 1) * HEAD_DIM, :].astype(BF16))
            out_ref[0, hd * VT_ROWS + HEAD_DIM:(hd + 1) * VT_ROWS, :] = ones
    lo, hi = _TSEG["wi"]
    wit_o[...] = zt[lo:hi, :] * (1.0 / 16.0)


def _proj(x2, g, w, wt, bf, tabs, consts, *, seq):
    n, d = x2.shape
    tm = TILE
    grid = (n // tm,)
    row = lambda i: (i, 0)
    fixed2 = lambda i: (0, 0)
    chunk3 = lambda i: (i, 0, 0)
    e_place, oq, ok = consts
    names = ("qa", "ka", "qb", "kb", "qi", "ki", "qc", "kc", "qm")
    out_specs = [pl.BlockSpec((tm, _SEG[nm][1] - _SEG[nm][0]), row) for nm in names]
    out_shape = [jax.ShapeDtypeStruct((n, _SEG[nm][1] - _SEG[nm][0]), BF16) for nm in names]
    for heads in (N_HEADS, N_HEADS, 1):
        out_specs.append(pl.BlockSpec((1, heads * VT_ROWS, tm), chunk3))
        out_shape.append(jax.ShapeDtypeStruct((n // tm, heads * VT_ROWS, tm), BF16))
    out_specs.append(pl.BlockSpec((BF16_ROWS, tm), lambda i: (0, i)))
    out_shape.append(jax.ShapeDtypeStruct((BF16_ROWS, n), F32))
    outs = pl.pallas_call(
        functools.partial(_proj_kernel, tiles_per_batch=seq // tm),
        grid=grid,
        in_specs=[pl.BlockSpec((tm, d), row), pl.BlockSpec((1, d), fixed2),
                  pl.BlockSpec((d, PROJ_COLS), fixed2), pl.BlockSpec((PROJ_T_ROWS, d), fixed2),
                  pl.BlockSpec((1, LANES), fixed2),
                  pl.BlockSpec((tm, LANES), row), pl.BlockSpec((tm, LANES), row),
                  pl.BlockSpec((tm, LANES), row), pl.BlockSpec((tm, LANES), row),
                  pl.BlockSpec((LANES, 1024), fixed2),
                  pl.BlockSpec((1, 512), fixed2), pl.BlockSpec((1, 512), fixed2)],
        out_specs=out_specs,
        out_shape=out_shape,
        scratch_shapes=[pltpu.VMEM((1, LANES), F32)],
        compiler_params=_cparams(("arbitrary",)),
        name="proj",
    )(x2, g, w, wt, bf, *tabs, e_place, oq, ok)
    return dict(zip(names + ("vaT", "vcT", "vbT", "wiT"), outs))


def _softmax_init(m_ref, acc_ref):
    m_ref[...] = jnp.full(m_ref.shape, NEG, F32)
    acc_ref[...] = jnp.zeros(acc_ref.shape, F32)


def _softmax_steps(streams, m_ref, acc_ref):
    n = len(streams)
    sts, alphas, ps = [None] * n, [None] * n, [None] * n

    def scores(k):
        sts[k] = streams[k][1]()

    def probs(k):
        idx = streams[k][0]
        m_prev = m_ref[idx]
        m_new = m_prev
        for st in sts[k]:
            m_new = jnp.maximum(m_new, jnp.max(st, axis=0, keepdims=True))
        alphas[k] = jnp.exp2(m_prev - m_new)
        ps[k] = [jnp.exp2(st - m_new).astype(BF16) for st in sts[k]]
        m_ref[idx] = m_new

    def values(k):
        idx, _, vts = streams[k]
        pv = _dot(vts[0], ps[k][0])
        for vt, p in zip(vts[1:], ps[k][1:]):
            pv = pv + _dot(vt, p)
        acc_ref[idx] = alphas[k] * acc_ref[idx] + pv

    for stage in (scores, probs, values):
        for k in range(n):
            stage(k)


def _pipelined_steps(n_steps, score_tiles, value_tiles, idxs, bufs, m_ref, acc_ref):
    def produce(n, st_buf, mx_buf):
        tiles = score_tiles(n)
        for k, sts in enumerate(tiles):
            mx = None
            for c, st in enumerate(sts):
                st_buf[k * len(sts) + c] = st
                cm = jnp.max(st, axis=0, keepdims=True)
                mx = cm if mx is None else jnp.maximum(mx, cm)
            mx_buf[k] = mx

    def consume(n, st_buf, mx_buf):
        vts = value_tiles(n)
        n_c = len(vts[0])
        alphas, ps = [], []
        for k, idx in enumerate(idxs):
            m_prev = m_ref[idx]
            m_new = jnp.maximum(m_prev, mx_buf[k])
            alphas.append(jnp.exp2(m_prev - m_new))
            ps.append([jnp.exp2(st_buf[k * n_c + c] - m_new).astype(BF16) for c in range(n_c)])
            m_ref[idx] = m_new
        for k, idx in enumerate(idxs):
            pv = _dot(vts[k][0], ps[k][0])
            for c in range(1, n_c):
                pv = pv + _dot(vts[k][c], ps[k][c])
            acc_ref[idx] = alphas[k] * acc_ref[idx] + pv

    _ping_pong(n_steps, produce, consume, *bufs)


def _ping_pong(n_steps, produce, consume, buf_a, buf_b):
    @pl.when(n_steps > 0)
    def _():
        produce(0, *buf_a)

    def body(s, c):
        produce(2 * s + 1, *buf_b)
        consume(2 * s, *buf_a)
        produce(2 * s + 2, *buf_a)
        consume(2 * s + 1, *buf_b)
        return c

    lax.fori_loop(0, n_steps // 2, body, 0)

    @pl.when(n_steps % 2 == 1)
    def _():
        consume(n_steps - 1, *buf_a)


def _normalized(acc_ref, idx):
    acc = acc_ref[idx]
    return acc[0:HEAD_DIM, :] / acc[HEAD_DIM:HEAD_DIM + 1, :]


def _store_pair(o_ref, g, o0, o1, scale_row=None):
    o = jnp.concatenate([o0, o1], axis=0).T
    if scale_row is not None:
        o = o * scale_row
    o_ref[:, g * LANES:(g + 1) * LANES] = o.astype(o_ref.dtype)


def _diag_mask(t):
    key = lax.broadcasted_iota(I32, (t, t), 0)
    qry = lax.broadcasted_iota(I32, (t, t), 1)
    return key <= qry


def _masked_heads(q_ref, qm_ref, width):
    tq = q_ref.shape[0]
    lane = lax.broadcasted_iota(I32, (tq, LANES), 1)
    per_group = LANES // width
    for idx in range(2 * per_group):
        g = idx // per_group
        qg = q_ref[:, g * LANES:(g + 1) * LANES]
        lo = (idx % per_group) * width
        sel = (lane >= lo) & (lane < lo + width)
        qm_ref[idx] = jnp.where(sel, qg, jnp.zeros_like(qg))


def _fox_kernel(q_ref, k_ref, vt_ref, o_ref, m_ref, acc_ref, sta_ref, mxa_ref, stb_ref, mxb_ref):
    i = pl.program_id(1)
    t = q_ref.shape[0]
    nq = vt_ref.shape[0]
    _softmax_init(m_ref, acc_ref)

    def chunks_of(n):
        return [jnp.minimum(j, nq - 1) for j in (2 * n, 2 * n + 1)]

    def score_tiles(n):
        return [[_dot_t(k_ref[pl.ds(pl.multiple_of(j * t, t), t), h * LANES:(h + 1) * LANES],
                        q_ref[:, h * LANES:(h + 1) * LANES]) for j in chunks_of(n)] for h in range(N_HEADS)]

    def value_tiles(n):
        return [[vt_ref[j, h * VT_ROWS:(h + 1) * VT_ROWS, :] for j in chunks_of(n)] for h in range(N_HEADS)]

    _pipelined_steps(i // 2, score_tiles, value_tiles, list(range(N_HEADS)),
                     ((sta_ref, mxa_ref), (stb_ref, mxb_ref)), m_ref, acc_ref)

    def last_step(js):
        streams = []
        for h in range(N_HEADS):
            def scores(h=h):
                q = q_ref[:, h * LANES:(h + 1) * LANES]
                sts = [_dot_t(k_ref[pl.ds(pl.multiple_of(j * t, t), t), h * LANES:(h + 1) * LANES], q)
                       for j in js]
                sts[-1] = jnp.where(_diag_mask(t), sts[-1], NEG)
                return sts
            streams.append((h, scores, [vt_ref[j, h * VT_ROWS:(h + 1) * VT_ROWS, :] for j in js]))
        _softmax_steps(streams, m_ref, acc_ref)

    @pl.when(i % 2 == 1)
    def _():
        last_step([i - 1, i])

    @pl.when(i % 2 == 0)
    def _():
        last_step([i])

    for g in range(N_HEADS // 2):
        _store_pair(o_ref, g, _normalized(acc_ref, 2 * g), _normalized(acc_ref, 2 * g + 1))


def _fox(qa, ka, vat, *, batch, seq):
    n = batch * seq
    t = TILE
    nq = seq // t
    return pl.pallas_call(
        _fox_kernel,
        grid=(batch, nq),
        in_specs=[pl.BlockSpec((t, 512), lambda b, i: (b * nq + i, 0)),
                  pl.BlockSpec((seq, 512), lambda b, i: (b, 0)),
                  pl.BlockSpec((nq, N_HEADS * VT_ROWS, t), lambda b, i: (b, 0, 0))],
        out_specs=pl.BlockSpec((t, 256), lambda b, i: (b * nq + i, 0)),
        out_shape=jax.ShapeDtypeStruct((n, 256), BF16),
        scratch_shapes=[pltpu.VMEM((N_HEADS, 1, t), F32), pltpu.VMEM((N_HEADS, VT_ROWS, t), F32),
                        pltpu.VMEM((2 * N_HEADS, t, t), F32), pltpu.VMEM((N_HEADS, 1, t), F32),
                        pltpu.VMEM((2 * N_HEADS, t, t), F32), pltpu.VMEM((N_HEADS, 1, t), F32)],
        compiler_params=_cparams(("arbitrary", "arbitrary")),
        name="fox",
    )(qa, ka, vat)


def _diff_kernel(q_ref, k_ref, vt_ref, dl_ref, li_ref, g_ref, o_ref, qm_ref, m_ref, acc_ref,
                 sta_ref, mxa_ref, stb_ref, mxb_ref):
    i = pl.program_id(1)
    t = q_ref.shape[0]
    nq = vt_ref.shape[0]
    n_streams = 2 * N_HEADS
    _softmax_init(m_ref, acc_ref)
    _masked_heads(q_ref, qm_ref, DIFF_QK_DIM)

    def chunks_of(n):
        return [jnp.minimum(j, nq - 1) for j in (2 * n, 2 * n + 1)]

    def score_tiles(n):
        tiles = []
        for g in range(N_HEADS // 2):
            ks = [k_ref[pl.ds(pl.multiple_of(j * t, t), t), g * LANES:(g + 1) * LANES] for j in chunks_of(n)]
            tiles += [[_dot_t(k, qm_ref[idx]) for k in ks] for idx in range(4 * g, 4 * g + 4)]
        return tiles

    def value_tiles(n):
        return [[vt_ref[j, (idx // 2) * VT_ROWS:(idx // 2 + 1) * VT_ROWS, :] for j in chunks_of(n)]
                for idx in range(n_streams)]

    _pipelined_steps(i // 2, score_tiles, value_tiles, list(range(n_streams)),
                     ((sta_ref, mxa_ref), (stb_ref, mxb_ref)), m_ref, acc_ref)

    def last_step(js):
        for g in range(N_HEADS // 2):
            ks = [k_ref[pl.ds(pl.multiple_of(j * t, t), t), g * LANES:(g + 1) * LANES] for j in js]
            streams = []
            for idx in range(4 * g, 4 * g + 4):
                def scores(idx=idx, ks=ks):
                    sts = [_dot_t(k, qm_ref[idx]) for k in ks]
                    sts[-1] = jnp.where(_diag_mask(t), sts[-1], NEG)
                    return sts
                h = idx // 2
                streams.append((idx, scores, [vt_ref[j, h * VT_ROWS:(h + 1) * VT_ROWS, :] for j in js]))
            _softmax_steps(streams, m_ref, acc_ref)

    @pl.when(i % 2 == 1)
    def _():
        last_step([i - 1, i])

    @pl.when(i % 2 == 0)
    def _():
        last_step([i])

    dl = dl_ref[...]
    lam_init = li_ref[...]
    lam = (jnp.exp(jnp.sum(dl[0:1] * dl[1:2], axis=-1, keepdims=True))
           - jnp.exp(jnp.sum(dl[2:3] * dl[3:4], axis=-1, keepdims=True)) + lam_init)
    scale_row = g_ref[...] * (1.0 - lam_init)
    for g in range(N_HEADS // 2):
        outs = []
        for h in (2 * g, 2 * g + 1):
            o = _normalized(acc_ref, 2 * h) - lam * _normalized(acc_ref, 2 * h + 1)
            outs.append(o * lax.rsqrt(jnp.mean(o * o, axis=0, keepdims=True) + NORM_EPS))
        _store_pair(o_ref, g, outs[0], outs[1], scale_row)


def _diff(qc, kc, vct, dl, lam_init, g2, *, batch, seq):
    n = batch * seq
    t = TILE
    nq = seq // t
    fixed = lambda b, i: (0, 0)
    return pl.pallas_call(
        _diff_kernel,
        grid=(batch, nq),
        in_specs=[pl.BlockSpec((t, 256), lambda b, i: (b * nq + i, 0)),
                  pl.BlockSpec((seq, 256), lambda b, i: (b, 0)),
                  pl.BlockSpec((nq, N_HEADS * VT_ROWS, t), lambda b, i: (b, 0, 0)),
                  pl.BlockSpec((4, DIFF_QK_DIM), fixed), pl.BlockSpec((1, 1), fixed),
                  pl.BlockSpec((1, LANES), fixed)],
        out_specs=pl.BlockSpec((t, 256), lambda b, i: (b * nq + i, 0)),
        out_shape=jax.ShapeDtypeStruct((n, 256), BF16),
        scratch_shapes=[pltpu.VMEM((2 * N_HEADS, t, LANES), BF16),
                        pltpu.VMEM((2 * N_HEADS, 1, t), F32),
                        pltpu.VMEM((2 * N_HEADS, VT_ROWS, t), F32),
                        pltpu.VMEM((4 * N_HEADS, t, t), F32), pltpu.VMEM((2 * N_HEADS, 1, t), F32),
                        pltpu.VMEM((4 * N_HEADS, t, t), F32), pltpu.VMEM((2 * N_HEADS, 1, t), F32)],
        compiler_params=_cparams(("arbitrary", "arbitrary")),
        name="diff",
    )(qc, kc, vct, dl, lam_init, g2)


def _mem_kernel(q_ref, mem_ref, g_ref, wk_ref, wvt_ref, o_ref, k_s, vt_s, qm_ref):
    i = pl.program_id(1)
    m_len = mem_ref.shape[1]

    @pl.when(i == 0)
    def _():
        hm = _rms(mem_ref[0], g_ref[...]).astype(BF16)
        k_s[...] = _dot(hm, wk_ref[...]).astype(BF16)
        vt = _dot_t(wvt_ref[...], hm)
        ones = _ones_rows(m_len)
        for h in range(N_HEADS):
            vt_s[h, 0:HEAD_DIM, :] = vt[h * HEAD_DIM:(h + 1) * HEAD_DIM, :].astype(BF16)
            vt_s[h, HEAD_DIM:VT_ROWS, :] = ones

    _masked_heads(q_ref, qm_ref, HEAD_DIM)
    sts = [_dot_t(k_s[:, (h // 2) * LANES:(h // 2 + 1) * LANES], qm_ref[h]) for h in range(N_HEADS)]
    ps = [jnp.exp2(st - jnp.max(st, axis=0, keepdims=True)).astype(BF16) for st in sts]
    accs = [_dot(vt_s[h], ps[h]) for h in range(N_HEADS)]
    outs = [acc[0:HEAD_DIM, :] / acc[HEAD_DIM:HEAD_DIM + 1, :] for acc in accs]
    for g in range(N_HEADS // 2):
        _store_pair(o_ref, g, outs[2 * g], outs[2 * g + 1])


def _mem_attn(qm, mem, g, wk, wvt, *, batch, seq):
    n = batch * seq
    t = TILE
    nq = seq // t
    m_len, d = mem.shape[1], mem.shape[2]
    fixed = lambda b, i: (0, 0)
    return pl.pallas_call(
        _mem_kernel,
        grid=(batch, nq),
        in_specs=[pl.BlockSpec((t, 256), lambda b, i: (b * nq + i, 0)),
                  pl.BlockSpec((1, m_len, d), lambda b, i: (b, 0, 0)),
                  pl.BlockSpec((1, d), fixed), pl.BlockSpec((d, 256), fixed), pl.BlockSpec((256, d), fixed)],
        out_specs=pl.BlockSpec((t, 256), lambda b, i: (b * nq + i, 0)),
        out_shape=jax.ShapeDtypeStruct((n, 256), BF16),
        scratch_shapes=[pltpu.VMEM((m_len, 256), BF16), pltpu.VMEM((N_HEADS, VT_ROWS, m_len), BF16),
                        pltpu.VMEM((N_HEADS, t, LANES), BF16)],
        compiler_params=_cparams(("arbitrary", "arbitrary")),
        name="mem_attn",
    )(qm, mem, g, wk, wvt)


def _store_bit_planes(key, planes_ref, j):
    tk = key.shape[0]
    assert tk == 8 * KEY_BITS, tk
    x = key ^ jnp.int32(INT_MIN)
    a = [x[8 * g:8 * g + 8, :] for g in range(KEY_BITS)]
    dist, mask = KEY_BITS // 2, 0x0000FFFF
    while dist:
        k = 0
        while k < KEY_BITS:
            swap = (a[k] ^ lax.shift_right_logical(a[k + dist], jnp.int32(dist))) & jnp.int32(mask)
            a[k] = a[k] ^ swap
            a[k + dist] = a[k + dist] ^ (swap << dist)
            k = (k + dist + 1) & ~dist
        dist >>= 1
        mask ^= mask << dist
    for b in range(KEY_BITS):
        planes_ref[j, b] = a[b]
    planes_ref[j, KEY_BITS] = jnp.full(a[0].shape, -1, I32)


def _dsa_kernel(qi_ref, wit_ref, qb_ref, ki_ref, kb_ref, vbt_ref, o_ref,
                keys_ref, planes_ref, cand_ref, qim_ref, qbm_ref, m_ref, acc_ref,
                sta_ref, mxa_ref, stb_ref, mxb_ref, *, top_k, idx_bits):
    i = pl.program_id(1)
    t = qi_ref.shape[0]
    n_ch = i + 1

    lane_q = lax.broadcasted_iota(I32, (t, 2 * LANES), 1)
    qi = qi_ref[...]
    for h in range(IDX_HEADS):
        qim_ref[h] = jnp.where((lane_q >> 5) == h, qi, jnp.zeros_like(qi))
    wit = wit_ref[...]

    def store_keys(j, score, masked):
        bits = lax.bitcast_convert_type(score, I32)
        key = bits ^ ((bits >> 31) & jnp.int32(0x7FFFFFFF))
        key = jnp.where(key == -1, 0, key)
        if masked:
            key = jnp.where(_diag_mask(t), key, jnp.int32(INT_MIN))
        keys_ref[j] = key
        _store_bit_planes(key, planes_ref, j)

    def last_score_step(js):
        kcs = [ki_ref[pl.ds(pl.multiple_of(j * t, t), t), :] for j in js]
        scores = [jnp.zeros((t, t), F32) for _ in js]
        for h in range(IDX_HEADS):
            w_h = wit[h:h + 1, :]
            scores = [sc + jnp.maximum(_dot_t(kc, qim_ref[h]), 0.0) * w_h for sc, kc in zip(scores, kcs)]
        for c, (j, score) in enumerate(zip(js, scores)):
            store_keys(j, score, c == len(js) - 1)

    n_kv = vbt_ref.shape[0]

    def raw_scores(n, buf):
        tiles = []
        for j in (2 * n, 2 * n + 1):
            kc = ki_ref[pl.ds(pl.multiple_of(jnp.minimum(j, n_kv - 1) * t, t), t), :]
            tiles += [_dot_t(kc, qim_ref[h]) for h in range(IDX_HEADS)]
        for k, tile in enumerate(tiles):
            buf[k] = tile

    def finish_scores(n, buf):
        for c, j in enumerate((2 * n, 2 * n + 1)):
            score = jnp.zeros((t, t), F32)
            for h in range(IDX_HEADS):
                score = score + jnp.maximum(buf[c * IDX_HEADS + h], 0.0) * wit[h:h + 1, :]
            store_keys(j, score, False)

    _ping_pong(i // 2, raw_scores, finish_scores, (sta_ref,), (stb_ref,))

    @pl.when(i % 2 == 1)
    def _():
        last_score_step([i - 1, i])

    @pl.when(i % 2 == 0)
    def _():
        last_score_step([i])
    for pad in range(KEY_PAD):
        keys_ref[n_ch + pad] = jnp.full((t, t), INT_MIN, I32)
        planes_ref[n_ch + pad, 0:KEY_BITS] = jnp.zeros((KEY_BITS, 8, t), I32)
        planes_ref[n_ch + pad, KEY_BITS] = jnp.full((8, t), -1, I32)
    n_pairs = (n_ch + 1) // 2

    def count(hits):
        def body(jj, acc):
            for j in (2 * jj, 2 * jj + 1):
                acc = acc + jnp.sum(hits(keys_ref[j], j).reshape(t // 8, 8, t), axis=0)
            return acc
        acc = lax.fori_loop(0, n_pairs, body, jnp.zeros((8, t), F32))
        return jnp.sum(acc, axis=0, keepdims=True)

    def bit_step(bi, carry):
        rem, n_cand, thr_bits, flip = carry
        prev = jnp.where(bi == 0, KEY_BITS, bi - 1)

        def body(jj, acc):
            for j in (2 * jj, 2 * jj + 1):
                cand = cand_ref[j] & (planes_ref[j, prev] ^ flip)
                cand_ref[j] = cand
                acc = acc + lax.population_count(cand & planes_ref[j, bi])
            return acc

        ones = lax.fori_loop(0, n_pairs, body, jnp.zeros((8, t), I32))
        n_ones = jnp.sum(ones.astype(F32), axis=0, keepdims=True)
        take = n_ones >= rem
        bit = jnp.int32(1) << (KEY_BITS - 1 - bi)
        return (jnp.where(take, rem, rem - n_ones), jnp.where(take, n_ones, n_cand - n_ones),
                jnp.where(take, thr_bits | bit, thr_bits), jnp.where(take, 0, -1))

    def reset_cand(j, c):
        cand_ref[j] = jnp.full((8, t), -1, I32)
        return c

    lax.fori_loop(0, 2 * n_pairs, reset_cand, 0)
    n_all = jnp.full((1, t), 2.0 * t, F32) * n_pairs.astype(F32)
    need, n_eq, thr_bits, _ = lax.fori_loop(
        0, KEY_BITS, bit_step, (jnp.full((1, t), float(top_k), F32), n_all,
                                jnp.zeros((1, t), I32), jnp.zeros((1, t), I32)))
    thr = thr_bits ^ jnp.int32(INT_MIN)

    tie = jnp.where(thr > INT_MIN, jnp.where(n_eq > need, 1.0, 0.0), 0.0)

    @pl.when(jnp.max(tie) > 0.0)
    def _():
        def key_pos(j):
            return j * t + lax.broadcasted_iota(I32, (t, t), 0)

        def idx_step(b, cut):
            cand = cut + (jnp.int32(1) << (idx_bits - 1 - b))
            cnt = count(lambda kc, j: jnp.where(kc == thr, jnp.where(key_pos(j) < cand, 1.0, 0.0), 0.0))
            return jnp.where(cnt < need, cand, cut)

        cut = lax.fori_loop(0, idx_bits, idx_step, jnp.zeros((1, t), I32))
        cut = jnp.where(tie > 0.0, cut, jnp.int32(2 ** 30))

        def demote(j, c):
            kc = keys_ref[j]
            demoted = jnp.where(key_pos(j) > cut, kc - 1, kc)
            keys_ref[j] = jnp.where(kc == thr, demoted, kc)
            return c

        lax.fori_loop(0, n_ch, demote, 0)

    thr_sel = jnp.maximum(thr, jnp.int32(INT_MIN + 1))

    _softmax_init(m_ref, acc_ref)
    _masked_heads(qb_ref, qbm_ref, HEAD_DIM)

    nq = vbt_ref.shape[0]

    def to_bias(j, c):
        bias = jnp.where(keys_ref[j] >= thr_sel, 0.0, NEG)
        keys_ref[j] = lax.bitcast_convert_type(bias, I32)
        return c

    lax.fori_loop(0, 2 * n_pairs, to_bias, 0)

    def chunks_of(n):
        return [(jnp.minimum(j, nq), jnp.minimum(j, nq - 1)) for j in (2 * n, 2 * n + 1)]

    def score_tiles(n):
        biases = [lax.bitcast_convert_type(keys_ref[jk], F32) for jk, _ in chunks_of(n)]
        ks = [kb_ref[pl.ds(pl.multiple_of(jv * t, t), t), :] for _, jv in chunks_of(n)]
        return [[_dot_t(k, qbm_ref[h]) + bias for bias, k in zip(biases, ks)] for h in range(N_HEADS)]

    def value_tiles(n):
        return [[vbt_ref[jv] for _, jv in chunks_of(n)]] * N_HEADS

    _pipelined_steps(n_pairs, score_tiles, value_tiles, list(range(N_HEADS)),
                     ((sta_ref, mxa_ref), (stb_ref, mxb_ref)), m_ref, acc_ref)
    for g in range(N_HEADS // 2):
        _store_pair(o_ref, g, _normalized(acc_ref, 2 * g), _normalized(acc_ref, 2 * g + 1))


def _dsa(qi, wit, qb, ki, kb, vbt, *, batch, seq):
    n = batch * seq
    t = TILE
    nq = seq // t
    top_k = min(IDX_TOPK_MAX, seq // 4)
    idx_bits = max(1, int(math.ceil(math.log2(seq))))
    qrow = lambda b, i: (b * nq + i, 0)
    kv = lambda b, i: (b, 0)
    return pl.pallas_call(
        functools.partial(_dsa_kernel, top_k=top_k, idx_bits=idx_bits),
        grid=(batch, nq),
        in_specs=[pl.BlockSpec((t, 256), qrow), pl.BlockSpec((BF16_ROWS, t), lambda b, i: (0, b * nq + i)),
                  pl.BlockSpec((t, 256), qrow),
                  pl.BlockSpec((seq, 256), kv), pl.BlockSpec((seq, LANES), kv),
                  pl.BlockSpec((nq, VT_ROWS, t), lambda b, i: (b, 0, 0))],
        out_specs=pl.BlockSpec((t, 256), qrow),
        out_shape=jax.ShapeDtypeStruct((n, 256), BF16),
        scratch_shapes=[pltpu.VMEM((nq + KEY_PAD, t, t), I32),
                        pltpu.VMEM((nq + KEY_PAD, KEY_BITS + 1, 8, t), I32),
                        pltpu.VMEM((nq + KEY_PAD, 8, t), I32),
                        pltpu.VMEM((IDX_HEADS, t, 2 * LANES), BF16),
                        pltpu.VMEM((N_HEADS, t, LANES), BF16),
                        pltpu.VMEM((N_HEADS, 1, t), F32), pltpu.VMEM((N_HEADS, VT_ROWS, t), F32),
                        pltpu.VMEM((2 * IDX_HEADS, t, t), F32), pltpu.VMEM((N_HEADS, 1, t), F32),
                        pltpu.VMEM((2 * IDX_HEADS, t, t), F32), pltpu.VMEM((N_HEADS, 1, t), F32)],
        compiler_params=_cparams(("arbitrary", "arbitrary")),
        name="dsa",
    )(qi, wit, qb, ki, kb, vbt)


def _merge_kernel(x_ref, g_ref, oa_ref, ob_ref, oc_ref, om_ref, wg_ref, bg_ref, wb_ref, wo_ref, y_ref):
    x = x_ref[...]
    d = x.shape[1]
    h = _rms(x, g_ref[...]).astype(BF16)
    acc = jnp.zeros(x.shape, F32)
    for nb, o_ref in enumerate((oa_ref, ob_ref, oc_ref, om_ref)):
        gate = jax.nn.sigmoid(_dot(h, wg_ref[:, nb * d:(nb + 1) * d]) + bg_ref[:, nb * d:(nb + 1) * d])
        acc = acc + gate * _dot(o_ref[...], wb_ref[nb])
    y_ref[...] = x + _dot(acc.astype(BF16), wo_ref[...])


def _merge(x2, g, oa, ob, oc, om, wg, bg, wb, wo):
    n, d = x2.shape
    tm = min(DENSE_ROWS, n)
    row = lambda i: (i, 0)
    fixed2 = lambda i: (0, 0)
    bw = oa.shape[1]
    once = pl.Buffered(1)
    return pl.pallas_call(
        _merge_kernel,
        grid=(n // tm,),
        in_specs=[pl.BlockSpec((tm, d), row), pl.BlockSpec((1, d), fixed2),
                  pl.BlockSpec((tm, bw), row), pl.BlockSpec((tm, bw), row),
                  pl.BlockSpec((tm, bw), row), pl.BlockSpec((tm, bw), row),
                  pl.BlockSpec((d, N_BRANCH * d), fixed2, pipeline_mode=once),
                  pl.BlockSpec((1, N_BRANCH * d), fixed2),
                  pl.BlockSpec((N_BRANCH, bw, d), lambda i: (0, 0, 0), pipeline_mode=once),
                  pl.BlockSpec((d, d), fixed2, pipeline_mode=once)],
        out_specs=pl.BlockSpec((tm, d), row),
        out_shape=jax.ShapeDtypeStruct((n, d), F32),
        compiler_params=_cparams(("arbitrary",)),
        name="merge",
    )(x2, g, oa, ob, oc, om, wg, bg, wb, wo)


def _ffn_kernel(x_ref, g_ref, wi_ref, wo_ref, gf_ref, y_ref, *, hidden, chunk, final):
    x = x_ref[...]
    h = _rms(x, g_ref[...]).astype(BF16)
    acc = jnp.zeros(x.shape, F32)
    for c in range(hidden // chunk):
        a = _dot(h, wi_ref[:, c * chunk:(c + 1) * chunk])
        b = _dot(h, wi_ref[:, hidden + c * chunk:hidden + (c + 1) * chunk])
        t = (a * jax.nn.sigmoid(a)) * b
        acc = acc + _dot(t.astype(BF16), wo_ref[c * chunk:(c + 1) * chunk, :])
    y = x + acc
    if final:
        y = _rms(y, gf_ref[...])
    y_ref[...] = y


def _ffn(x2, g, w_in, w_out, g_final, *, final):
    n, d = x2.shape
    tm = min(DENSE_ROWS, n)
    hidden = w_out.shape[0]
    chunk = hidden // 4 if (hidden // 4) % LANES == 0 else hidden
    row = lambda i: (i, 0)
    fixed2 = lambda i: (0, 0)
    once = pl.Buffered(1)
    return pl.pallas_call(
        functools.partial(_ffn_kernel, hidden=hidden, chunk=chunk, final=final),
        grid=(n // tm,),
        in_specs=[pl.BlockSpec((tm, d), row), pl.BlockSpec((1, d), fixed2),
                  pl.BlockSpec((d, 2 * hidden), fixed2, pipeline_mode=once),
                  pl.BlockSpec((hidden, d), fixed2, pipeline_mode=once),
                  pl.BlockSpec((1, d), fixed2)],
        out_specs=pl.BlockSpec((tm, d), row),
        out_shape=jax.ShapeDtypeStruct((n, d), F32),
        compiler_params=_cparams(("arbitrary",)),
        name="ffn",
    )(x2, g, w_in, w_out, g_final)


_IN_SPLITS = (256, 256, 256, 4, 256, 64, 64, 256, 32, 8, 256, 256, 256, 256)


def _pack_w_in(w):
    d = w.shape[0]
    offs = np.cumsum((0,) + _IN_SPLITS)
    qa, ka, va, fa, qb, kb, vb, qi, ki, wi, qc, kc, vc, qm = (
        w[:, offs[k]:offs[k + 1]] for k in range(len(_IN_SPLITS)))

    def pad_heads(a):
        a = a.reshape(d, N_HEADS, HEAD_DIM)
        return jnp.pad(a, ((0, 0), (0, 0), (0, LANES - HEAD_DIM))).reshape(d, N_HEADS * LANES)

    def pad_to(a, wd):
        return jnp.pad(a, ((0, 0), (0, wd - a.shape[1])))

    cols = [pad_heads(qa), pad_heads(ka), pad_to(jnp.tile(fa, (1, 3)), LANES), qb,
            jnp.concatenate([kb, kb], axis=1),
            qi, jnp.tile(ki, (1, IDX_HEADS)), qc, kc, qm]
    rows = [va, vc, vb, pad_to(wi, BF16_ROWS)]
    return (jnp.concatenate(cols, axis=1).astype(BF16), jnp.concatenate(rows, axis=1).T.astype(BF16))


def _fox_consts():
    e = np.zeros((LANES, 1024), np.float32)
    oq = np.zeros((1, 512), np.float32)
    ok = np.zeros((1, 512), np.float32)
    for h in range(N_HEADS):
        base = h * LANES + HEAD_DIM
        for t in range(3):
            e[N_HEADS * t + h, base + t] = 1.0
            e[N_HEADS * t + h, 512 + base + 3 + t] = -1.0
            oq[0, base + 3 + t] = 1.0
            ok[0, base + t] = 1.0
    return (jnp.asarray(e, BF16), jnp.asarray(oq), jnp.asarray(ok))


def _rope_tables(positions, rot_dim, period):
    half = rot_dim // 2
    inv_freq = ROPE_THETA ** (-jnp.arange(0, rot_dim, 2, dtype=F32) / rot_dim)
    ang = positions.astype(F32).reshape(-1, 1) * inv_freq
    cos, sin = jnp.cos(ang), jnp.sin(ang)
    n = ang.shape[0]
    c = jnp.concatenate([cos, cos, jnp.ones((n, period - 2 * half), F32)], axis=1)
    s = jnp.concatenate([-sin, sin, jnp.zeros((n, period - 2 * half), F32)], axis=1)
    reps = LANES // period
    return jnp.tile(c, (1, reps)), jnp.tile(s, (1, reps))


def kernel(x, mem, positions, norm_mix_g, w_in, b_forget, diff_lambda, diff_subln_g, norm_mem_g,
           w_mem_kv, w_branch, w_gate, b_gate, w_out, norm_ffn_g, w_ffn_in, w_ffn_out, final_norm_g):
    batch, seq, d = x.shape
    depth = w_in.shape[0]
    n = batch * seq
    assert seq % TILE == 0, seq

    c16, s16 = _rope_tables(positions, HEAD_DIM // 4, HEAD_DIM)
    c8, s8 = _rope_tables(positions, DIFF_QK_DIM // 4, DIFF_QK_DIM)
    consts = _fox_consts()

    x2 = x.reshape(n, d)
    for l in range(depth):
        lam_init = 0.8 - 0.6 * math.exp(-0.3 * l)
        bf = jnp.pad(jnp.tile(b_forget[l], 3), (0, LANES - 3 * b_forget.shape[1])).reshape(1, LANES)
        w_row, w_t = _pack_w_in(w_in[l])
        z = _proj(x2, norm_mix_g[l].reshape(1, d), w_row, w_t, bf, (c16, s16, c8, s8), consts, seq=seq)
        o_a = _fox(z["qa"], z["ka"], z["vaT"], batch=batch, seq=seq)
        o_b = _dsa(z["qi"], z["wiT"], z["qb"], z["ki"], z["kb"], z["vbT"], batch=batch, seq=seq)
        o_c = _diff(z["qc"], z["kc"], z["vcT"], diff_lambda[l], jnp.full((1, 1), lam_init, F32),
                    jnp.tile(diff_subln_g[l], 2).reshape(1, LANES), batch=batch, seq=seq)
        w_kv = w_mem_kv[l]
        o_m = _mem_attn(z["qm"], mem, norm_mem_g[l].reshape(1, d), w_kv[:, :256].astype(BF16),
                        w_kv[:, 256:].T.astype(BF16), batch=batch, seq=seq)
        x2 = _merge(x2, norm_mix_g[l].reshape(1, d), o_a, o_b, o_c, o_m,
                    w_gate[l].astype(BF16), b_gate[l].reshape(1, -1), w_branch[l].astype(BF16),
                    w_out[l].astype(BF16))
        x2 = _ffn(x2, norm_ffn_g[l].reshape(1, d), w_ffn_in[l].astype(BF16), w_ffn_out[l].astype(BF16),
                  final_norm_g.reshape(1, d), final=(l == depth - 1))
    return x2.reshape(batch, seq, d)
```

```python
import functools
import math

import numpy as np
import jax
import jax.numpy as jnp
from jax import lax
from jax.experimental import pallas as pl
from jax.experimental.pallas import tpu as pltpu

F32 = jnp.float32
BF16 = jnp.bfloat16
I32 = jnp.int32

HEAD_DIM = 64
N_BRANCH = 4
N_HEADS = 4
IDX_HEADS = 8
IDX_DIM = 32
IDX_TOPK_MAX = 256
DIFF_QK_DIM = 32
ROPE_THETA = 500000.0
NORM_EPS = 1e-6
LANES = 128
BF16_ROWS = 16
NEG = -1e30
LOG2E = math.log2(math.e)
INT_MIN = -2 ** 31
TILE = 256
DENSE_ROWS = 1024
MEM_ROWS = 512
VT_ROWS = HEAD_DIM + BF16_ROWS
KEY_BITS = 32
KEY_PAD = 1

VMEM_LIMIT = 56 * 1024 * 1024

_SEG = {}
_off = 0
for _name, _w in (("qa", 512), ("ka", 512), ("fa", 128), ("qb", 256), ("kb", 128),
                  ("qi", 256), ("ki", 256), ("qc", 256), ("kc", 256), ("qm", 256)):
    _SEG[_name] = (_off, _off + _w)
    _off += _w
PROJ_COLS = _off
_TSEG = {"va": (0, 256), "vc": (256, 512), "vb": (512, 576), "wi": (576, 592)}
PROJ_T_ROWS = 592

_DN_T = (((1,), (1,)), ((), ()))


def _dot(a, b):
    return jnp.dot(a, b, preferred_element_type=F32)


def _dot_t(a, b):
    return lax.dot_general(a, b, _DN_T, preferred_element_type=F32)


def _rms(x, g):
    return x * lax.rsqrt(jnp.mean(x * x, axis=-1, keepdims=True) + NORM_EPS) * g


def _cparams(sem):
    return pltpu.CompilerParams(dimension_semantics=sem, vmem_limit_bytes=VMEM_LIMIT)


def _split3(x):
    hi = x.astype(BF16)
    r = x - hi.astype(F32)
    mid = r.astype(BF16)
    lo = (r - mid.astype(F32)).astype(BF16)
    return hi, mid, lo


def _ones_rows(cols):
    r = lax.broadcasted_iota(I32, (BF16_ROWS, cols), 0)
    return jnp.where(r == 0, 1.0, 0.0).astype(BF16)


def _rope(z, c, s, half, period):
    lane = lax.broadcasted_iota(I32, z.shape, 1)
    up = pltpu.roll(z, LANES - half, 1)
    dn = pltpu.roll(z, half, 1)
    partner = jnp.where((lane & (period - 1)) < half, up, dn)
    return z * c + partner * s


def _proj_kernel(x_ref, g_ref, w_ref, wt_ref, bf_ref, c16_ref, s16_ref, c8_ref, s8_ref, e_ref,
                 oq_ref, ok_ref,
                 qa_o, ka_o, qb_o, kb_o, qi_o, ki_o, qc_o, kc_o, qm_o, vat_o, vct_o, vbt_o, wit_o,
                 carry_ref, *, tiles_per_batch):
    i = pl.program_id(0)
    tm = x_ref.shape[0]
    h = _rms(x_ref[...], g_ref[...]).astype(BF16)

    def mm(name):
        lo, hi = _SEG[name]
        return _dot(h, w_ref[:, lo:hi])

    f = mm("fa") + bf_ref[...]
    ls = jnp.minimum(f, 0.0) - jnp.log(1.0 + jnp.exp(-jnp.abs(f)))
    r_i = lax.broadcasted_iota(I32, (tm, tm), 0)
    c_i = lax.broadcasted_iota(I32, (tm, tm), 1)
    ltri = jnp.where(c_i <= r_i, 1.0, 0.0).astype(BF16)
    p_hi, p_mid, p_lo = _split3(ls)
    pre = _dot(ltri, p_hi) + _dot(ltri, p_mid) + _dot(ltri, p_lo)

    @pl.when(i % tiles_per_batch == 0)
    def _():
        carry_ref[...] = jnp.zeros_like(carry_ref)

    cum = pre + carry_ref[...]
    carry_ref[...] = cum[tm - 1:tm, :]
    c_hi, c_mid, c_lo = _split3(cum * LOG2E)
    lane = lax.broadcasted_iota(I32, cum.shape, 1)
    c_terms = jnp.where(lane < N_HEADS, c_hi, jnp.where(lane < 2 * N_HEADS, c_mid, c_lo))

    placed = _dot(c_terms, e_ref[...])
    qa = mm("qa") * (HEAD_DIM ** -0.5 * LOG2E) + placed[:, :512] + oq_ref[...]
    qa_o[...] = qa.astype(BF16)
    ka = mm("ka") + placed[:, 512:] + ok_ref[...]
    ka_o[...] = ka.astype(BF16)

    c16 = c16_ref[...]
    s16 = s16_ref[...]
    c8 = c8_ref[...]
    s8 = s8_ref[...]

    def roped(name, out_ref, c, s, half, period, scale):
        z = mm(name)
        for k in range(z.shape[1] // LANES):
            r = _rope(z[:, k * LANES:(k + 1) * LANES], c, s, half, period)
            if scale != 1.0:
                r = r * scale
            out_ref[:, k * LANES:(k + 1) * LANES] = r.astype(BF16)

    roped("qb", qb_o, c16, s16, 8, 64, HEAD_DIM ** -0.5 * LOG2E)
    roped("kb", kb_o, c16, s16, 8, 64, 1.0)
    roped("qi", qi_o, c8, s8, 4, 32, 1.0)
    roped("ki", ki_o, c8, s8, 4, 32, 1.0)
    roped("qc", qc_o, c8, s8, 4, 32, DIFF_QK_DIM ** -0.5 * LOG2E)
    roped("kc", kc_o, c8, s8, 4, 32, 1.0)
    qm_o[...] = (mm("qm") * (HEAD_DIM ** -0.5 * LOG2E)).astype(BF16)

    zt = _dot_t(wt_ref[...], h)
    ones = _ones_rows(tm)
    for name, out_ref, heads in (("va", vat_o, N_HEADS), ("vc", vct_o, N_HEADS), ("vb", vbt_o, 1)):
        lo = _TSEG[name][0]
        for hd in range(heads):
            out_ref[0, hd * VT_ROWS:hd * VT_ROWS + HEAD_DIM, :] = (
                zt[lo + hd * HEAD_DIM:lo + (hd + 1) * HEAD_DIM, :].astype(BF16))
            out_ref[0, hd * VT_ROWS + HEAD_DIM:(hd + 1) * VT_ROWS, :] = ones
    lo, hi = _TSEG["wi"]
    wit_o[...] = zt[lo:hi, :] * (1.0 / 16.0)


def _proj(x2, g, w, wt, bf, tabs, consts, *, seq):
    n, d = x2.shape
    tm = TILE
    grid = (n // tm,)
    row = lambda i: (i, 0)
    fixed2 = lambda i: (0, 0)
    chunk3 = lambda i: (i, 0, 0)
    e_place, oq, ok = consts
    names = ("qa", "ka", "qb", "kb", "qi", "ki", "qc", "kc", "qm")
    out_specs = [pl.BlockSpec((tm, _SEG[nm][1] - _SEG[nm][0]), row) for nm in names]
    out_shape = [jax.ShapeDtypeStruct((n, _SEG[nm][1] - _SEG[nm][0]), BF16) for nm in names]
    for heads in (N_HEADS, N_HEADS, 1):
        out_specs.append(pl.BlockSpec((1, heads * VT_ROWS, tm), chunk3))
        out_shape.append(jax.ShapeDtypeStruct((n // tm, heads * VT_ROWS, tm), BF16))
    out_specs.append(pl.BlockSpec((BF16_ROWS, tm), lambda i: (0, i)))
    out_shape.append(jax.ShapeDtypeStruct((BF16_ROWS, n), F32))
    outs = pl.pallas_call(
        functools.partial(_proj_kernel, tiles_per_batch=seq // tm),
        grid=grid,
        in_specs=[pl.BlockSpec((tm, d), row), pl.BlockSpec((1, d), fixed2),
                  pl.BlockSpec((d, PROJ_COLS), fixed2), pl.BlockSpec((PROJ_T_ROWS, d), fixed2),
                  pl.BlockSpec((1, LANES), fixed2),
                  pl.BlockSpec((tm, LANES), row), pl.BlockSpec((tm, LANES), row),
                  pl.BlockSpec((tm, LANES), row), pl.BlockSpec((tm, LANES), row),
                  pl.BlockSpec((LANES, 1024), fixed2),
                  pl.BlockSpec((1, 512), fixed2), pl.BlockSpec((1, 512), fixed2)],
        out_specs=out_specs,
        out_shape=out_shape,
        scratch_shapes=[pltpu.VMEM((1, LANES), F32)],
        compiler_params=_cparams(("arbitrary",)),
        name="proj",
    )(x2, g, w, wt, bf, *tabs, e_place, oq, ok)
    return dict(zip(names + ("vaT", "vcT", "vbT", "wiT"), outs))


def _softmax_init(m_ref, acc_ref):
    m_ref[...] = jnp.full(m_ref.shape, NEG, F32)
    acc_ref[...] = jnp.zeros(acc_ref.shape, F32)


def _softmax_steps(streams, m_ref, acc_ref):
    n = len(streams)
    sts, alphas, ps = [None] * n, [None] * n, [None] * n

    def scores(k):
        sts[k] = streams[k][1]()

    def probs(k):
        idx = streams[k][0]
        m_prev = m_ref[idx]
        m_new = m_prev
        for st in sts[k]:
            m_new = jnp.maximum(m_new, jnp.max(st, axis=0, keepdims=True))
        alphas[k] = jnp.exp2(m_prev - m_new)
        ps[k] = [jnp.exp2(st - m_new).astype(BF16) for st in sts[k]]
        m_ref[idx] = m_new

    def values(k):
        idx, _, vts = streams[k]
        pv = _dot(vts[0], ps[k][0])
        for vt, p in zip(vts[1:], ps[k][1:]):
            pv = pv + _dot(vt, p)
        acc_ref[idx] = alphas[k] * acc_ref[idx] + pv

    for stage in (scores, probs, values):
        for k in range(n):
            stage(k)


def _pipelined_steps(n_steps, score_tiles, value_tiles, idxs, bufs, m_ref, acc_ref):
    def produce(n, st_buf, mx_buf):
        tiles = score_tiles(n)
        for k, sts in enumerate(tiles):
            mx = None
            for c, st in enumerate(sts):
                st_buf[k * len(sts) + c] = st
                cm = jnp.max(st, axis=0, keepdims=True)
                mx = cm if mx is None else jnp.maximum(mx, cm)
            mx_buf[k] = mx

    def consume(n, st_buf, mx_buf):
        vts = value_tiles(n)
        n_c = len(vts[0])
        alphas, ps = [], []
        for k, idx in enumerate(idxs):
            m_prev = m_ref[idx]
            m_new = jnp.maximum(m_prev, mx_buf[k])
            alphas.append(jnp.exp2(m_prev - m_new))
            ps.append([jnp.exp2(st_buf[k * n_c + c] - m_new).astype(BF16) for c in range(n_c)])
            m_ref[idx] = m_new
        for k, idx in enumerate(idxs):
            pv = _dot(vts[k][0], ps[k][0])
            for c in range(1, n_c):
                pv = pv + _dot(vts[k][c], ps[k][c])
            acc_ref[idx] = alphas[k] * acc_ref[idx] + pv

    _ping_pong(n_steps, produce, consume, *bufs)


def _ping_pong(n_steps, produce, consume, buf_a, buf_b):
    @pl.when(n_steps > 0)
    def _():
        produce(0, *buf_a)

    def body(s, c):
        produce(2 * s + 1, *buf_b)
        consume(2 * s, *buf_a)
        produce(2 * s + 2, *buf_a)
        consume(2 * s + 1, *buf_b)
        return c

    lax.fori_loop(0, n_steps // 2, body, 0)

    @pl.when(n_steps % 2 == 1)
    def _():
        consume(n_steps - 1, *buf_a)


def _normalized(acc_ref, idx):
    acc = acc_ref[idx]
    return acc[0:HEAD_DIM, :] / acc[HEAD_DIM:HEAD_DIM + 1, :]


def _store_pair(o_ref, g, o0, o1, scale_row=None):
    o = jnp.concatenate([o0, o1], axis=0).T
    if scale_row is not None:
        o = o * scale_row
    o_ref[:, g * LANES:(g + 1) * LANES] = o.astype(o_ref.dtype)


def _diag_mask(t):
    key = lax.broadcasted_iota(I32, (t, t), 0)
    qry = lax.broadcasted_iota(I32, (t, t), 1)
    return key <= qry


def _masked_heads(q_ref, qm_ref, width):
    tq = q_ref.shape[0]
    lane = lax.broadcasted_iota(I32, (tq, LANES), 1)
    per_group = LANES // width
    for idx in range(2 * per_group):
        g = idx // per_group
        qg = q_ref[:, g * LANES:(g + 1) * LANES]
        lo = (idx % per_group) * width
        sel = (lane >= lo) & (lane < lo + width)
        qm_ref[idx] = jnp.where(sel, qg, jnp.zeros_like(qg))


def _fox_kernel(q_ref, k_ref, vt_ref, o_ref, m_ref, acc_ref, sta_ref, mxa_ref, stb_ref, mxb_ref):
    i = pl.program_id(1)
    t = q_ref.shape[0]
    nq = vt_ref.shape[0]
    _softmax_init(m_ref, acc_ref)

    def chunks_of(n):
        return [jnp.minimum(j, nq - 1) for j in (2 * n, 2 * n + 1)]

    def score_tiles(n):
        return [[_dot_t(k_ref[pl.ds(pl.multiple_of(j * t, t), t), h * LANES:(h + 1) * LANES],
                        q_ref[:, h * LANES:(h + 1) * LANES]) for j in chunks_of(n)] for h in range(N_HEADS)]

    def value_tiles(n):
        return [[vt_ref[j, h * VT_ROWS:(h + 1) * VT_ROWS, :] for j in chunks_of(n)] for h in range(N_HEADS)]

    _pipelined_steps(i // 2, score_tiles, value_tiles, list(range(N_HEADS)),
                     ((sta_ref, mxa_ref), (stb_ref, mxb_ref)), m_ref, acc_ref)

    def last_step(js):
        streams = []
        for h in range(N_HEADS):
            def scores(h=h):
                q = q_ref[:, h * LANES:(h + 1) * LANES]
                sts = [_dot_t(k_ref[pl.ds(pl.multiple_of(j * t, t), t), h * LANES:(h + 1) * LANES], q)
                       for j in js]
                sts[-1] = jnp.where(_diag_mask(t), sts[-1], NEG)
                return sts
            streams.append((h, scores, [vt_ref[j, h * VT_ROWS:(h + 1) * VT_ROWS, :] for j in js]))
        _softmax_steps(streams, m_ref, acc_ref)

    @pl.when(i % 2 == 1)
    def _():
        last_step([i - 1, i])

    @pl.when(i % 2 == 0)
    def _():
        last_step([i])

    for g in range(N_HEADS // 2):
        _store_pair(o_ref, g, _normalized(acc_ref, 2 * g), _normalized(acc_ref, 2 * g + 1))


def _fox(qa, ka, vat, *, batch, seq):
    n = batch * seq
    t = TILE
    nq = seq // t
    return pl.pallas_call(
        _fox_kernel,
        grid=(batch, nq),
        in_specs=[pl.BlockSpec((t, 512), lambda b, i: (b * nq + i, 0)),
                  pl.BlockSpec((seq, 512), lambda b, i: (b, 0)),
                  pl.BlockSpec((nq, N_HEADS * VT_ROWS, t), lambda b, i: (b, 0, 0))],
        out_specs=pl.BlockSpec((t, 256), lambda b, i: (b * nq + i, 0)),
        out_shape=jax.ShapeDtypeStruct((n, 256), BF16),
        scratch_shapes=[pltpu.VMEM((N_HEADS, 1, t), F32), pltpu.VMEM((N_HEADS, VT_ROWS, t), F32),
                        pltpu.VMEM((2 * N_HEADS, t, t), F32), pltpu.VMEM((N_HEADS, 1, t), F32),
                        pltpu.VMEM((2 * N_HEADS, t, t), F32), pltpu.VMEM((N_HEADS, 1, t), F32)],
        compiler_params=_cparams(("arbitrary", "arbitrary")),
        name="fox",
    )(qa, ka, vat)


def _diff_kernel(q_ref, k_ref, vt_ref, dl_ref, li_ref, g_ref, o_ref, qm_ref, m_ref, acc_ref,
                 sta_ref, mxa_ref, stb_ref, mxb_ref):
    i = pl.program_id(1)
    t = q_ref.shape[0]
    nq = vt_ref.shape[0]
    n_streams = 2 * N_HEADS
    _softmax_init(m_ref, acc_ref)
    _masked_heads(q_ref, qm_ref, DIFF_QK_DIM)

    def chunks_of(n):
        return [jnp.minimum(j, nq - 1) for j in (2 * n, 2 * n + 1)]

    def score_tiles(n):
        tiles = []
        for g in range(N_HEADS // 2):
            ks = [k_ref[pl.ds(pl.multiple_of(j * t, t), t), g * LANES:(g + 1) * LANES] for j in chunks_of(n)]
            tiles += [[_dot_t(k, qm_ref[idx]) for k in ks] for idx in range(4 * g, 4 * g + 4)]
        return tiles

    def value_tiles(n):
        return [[vt_ref[j, (idx // 2) * VT_ROWS:(idx // 2 + 1) * VT_ROWS, :] for j in chunks_of(n)]
                for idx in range(n_streams)]

    _pipelined_steps(i // 2, score_tiles, value_tiles, list(range(n_streams)),
                     ((sta_ref, mxa_ref), (stb_ref, mxb_ref)), m_ref, acc_ref)

    def last_step(js):
        for g in range(N_HEADS // 2):
            ks = [k_ref[pl.ds(pl.multiple_of(j * t, t), t), g * LANES:(g + 1) * LANES] for j in js]
            streams = []
            for idx in range(4 * g, 4 * g + 4):
                def scores(idx=idx, ks=ks):
                    sts = [_dot_t(k, qm_ref[idx]) for k in ks]
                    sts[-1] = jnp.where(_diag_mask(t), sts[-1], NEG)
                    return sts
                h = idx // 2
                streams.append((idx, scores, [vt_ref[j, h * VT_ROWS:(h + 1) * VT_ROWS, :] for j in js]))
            _softmax_steps(streams, m_ref, acc_ref)

    @pl.when(i % 2 == 1)
    def _():
        last_step([i - 1, i])

    @pl.when(i % 2 == 0)
    def _():
        last_step([i])

    dl = dl_ref[...]
    lam_init = li_ref[...]
    lam = (jnp.exp(jnp.sum(dl[0:1] * dl[1:2], axis=-1, keepdims=True))
           - jnp.exp(jnp.sum(dl[2:3] * dl[3:4], axis=-1, keepdims=True)) + lam_init)
    scale_row = g_ref[...] * (1.0 - lam_init)
    for g in range(N_HEADS // 2):
        outs = []
        for h in (2 * g, 2 * g + 1):
            o = _normalized(acc_ref, 2 * h) - lam * _normalized(acc_ref, 2 * h + 1)
            outs.append(o * lax.rsqrt(jnp.mean(o * o, axis=0, keepdims=True) + NORM_EPS))
        _store_pair(o_ref, g, outs[0], outs[1], scale_row)


def _diff(qc, kc, vct, dl, lam_init, g2, *, batch, seq):
    n = batch * seq
    t = TILE
    nq = seq // t
    fixed = lambda b, i: (0, 0)
    return pl.pallas_call(
        _diff_kernel,
        grid=(batch, nq),
        in_specs=[pl.BlockSpec((t, 256), lambda b, i: (b * nq + i, 0)),
                  pl.BlockSpec((seq, 256), lambda b, i: (b, 0)),
                  pl.BlockSpec((nq, N_HEADS * VT_ROWS, t), lambda b, i: (b, 0, 0)),
                  pl.BlockSpec((4, DIFF_QK_DIM), fixed), pl.BlockSpec((1, 1), fixed),
                  pl.BlockSpec((1, LANES), fixed)],
        out_specs=pl.BlockSpec((t, 256), lambda b, i: (b * nq + i, 0)),
        out_shape=jax.ShapeDtypeStruct((n, 256), BF16),
        scratch_shapes=[pltpu.VMEM((2 * N_HEADS, t, LANES), BF16),
                        pltpu.VMEM((2 * N_HEADS, 1, t), F32),
                        pltpu.VMEM((2 * N_HEADS, VT_ROWS, t), F32),
                        pltpu.VMEM((4 * N_HEADS, t, t), F32), pltpu.VMEM((2 * N_HEADS, 1, t), F32),
                        pltpu.VMEM((4 * N_HEADS, t, t), F32), pltpu.VMEM((2 * N_HEADS, 1, t), F32)],
        compiler_params=_cparams(("arbitrary", "arbitrary")),
        name="diff",
    )(qc, kc, vct, dl, lam_init, g2)


def _mem_kernel(q_ref, mem_ref, g_ref, wk_ref, wvt_ref, o_ref, k_s, vt_s, qm_ref):
    i = pl.program_id(1)
    m_len = mem_ref.shape[1]

    @pl.when(i == 0)
    def _():
        hm = _rms(mem_ref[0], g_ref[...]).astype(BF16)
        k_s[...] = _dot(hm, wk_ref[...]).astype(BF16)
        vt = _dot_t(wvt_ref[...], hm)
        ones = _ones_rows(m_len)
        for h in range(N_HEADS):
            vt_s[h, 0:HEAD_DIM, :] = vt[h * HEAD_DIM:(h + 1) * HEAD_DIM, :].astype(BF16)
            vt_s[h, HEAD_DIM:VT_ROWS, :] = ones

    _masked_heads(q_ref, qm_ref, HEAD_DIM)
    sts = [_dot_t(k_s[:, (h // 2) * LANES:(h // 2 + 1) * LANES], qm_ref[h]) for h in range(N_HEADS)]
    ps = [jnp.exp2(st - jnp.max(st, axis=0, keepdims=True)).astype(BF16) for st in sts]
    accs = [_dot(vt_s[h], ps[h]) for h in range(N_HEADS)]
    outs = [acc[0:HEAD_DIM, :] / acc[HEAD_DIM:HEAD_DIM + 1, :] for acc in accs]
    for g in range(N_HEADS // 2):
        _store_pair(o_ref, g, outs[2 * g], outs[2 * g + 1])


def _mem_attn(qm, mem, g, wk, wvt, *, batch, seq):
    n = batch * seq
    t = MEM_ROWS if seq % MEM_ROWS == 0 else TILE
    nq = seq // t
    m_len, d = mem.shape[1], mem.shape[2]
    fixed = lambda b, i: (0, 0)
    return pl.pallas_call(
        _mem_kernel,
        grid=(batch, nq),
        in_specs=[pl.BlockSpec((t, 256), lambda b, i: (b * nq + i, 0)),
                  pl.BlockSpec((1, m_len, d), lambda b, i: (b, 0, 0)),
                  pl.BlockSpec((1, d), fixed), pl.BlockSpec((d, 256), fixed), pl.BlockSpec((256, d), fixed)],
        out_specs=pl.BlockSpec((t, 256), lambda b, i: (b * nq + i, 0)),
        out_shape=jax.ShapeDtypeStruct((n, 256), BF16),
        scratch_shapes=[pltpu.VMEM((m_len, 256), BF16), pltpu.VMEM((N_HEADS, VT_ROWS, m_len), BF16),
                        pltpu.VMEM((N_HEADS, t, LANES), BF16)],
        compiler_params=_cparams(("arbitrary", "arbitrary")),
        name="mem_attn",
    )(qm, mem, g, wk, wvt)


def _store_bit_planes(key, planes_ref, j):
    tk = key.shape[0]
    assert tk == 8 * KEY_BITS, tk
    x = key ^ jnp.int32(INT_MIN)
    a = [x[8 * g:8 * g + 8, :] for g in range(KEY_BITS)]
    dist, mask = KEY_BITS // 2, 0x0000FFFF
    while dist:
        k = 0
        while k < KEY_BITS:
            swap = (a[k] ^ lax.shift_right_logical(a[k + dist], jnp.int32(dist))) & jnp.int32(mask)
            a[k] = a[k] ^ swap
            a[k + dist] = a[k + dist] ^ (swap << dist)
            k = (k + dist + 1) & ~dist
        dist >>= 1
        mask ^= mask << dist
    for b in range(KEY_BITS):
        planes_ref[j, b] = a[b]
    planes_ref[j, KEY_BITS] = jnp.full(a[0].shape, -1, I32)


def _dsa_kernel(qi_ref, wit_ref, qb_ref, ki_ref, kb_ref, vbt_ref, o_ref,
                keys_ref, planes_ref, cand_ref, qim_ref, qbm_ref, m_ref, acc_ref,
                sta_ref, mxa_ref, stb_ref, mxb_ref, *, top_k, idx_bits):
    i = pl.program_id(1)
    t = qi_ref.shape[0]
    n_ch = i + 1

    lane_q = lax.broadcasted_iota(I32, (t, 2 * LANES), 1)
    qi = qi_ref[...]
    for h in range(IDX_HEADS):
        qim_ref[h] = jnp.where((lane_q >> 5) == h, qi, jnp.zeros_like(qi))
    wit = wit_ref[...]

    def store_keys(j, score, masked):
        bits = lax.bitcast_convert_type(score, I32)
        key = bits ^ ((bits >> 31) & jnp.int32(0x7FFFFFFF))
        key = jnp.where(key == -1, 0, key)
        if masked:
            key = jnp.where(_diag_mask(t), key, jnp.int32(INT_MIN))
        keys_ref[j] = key
        _store_bit_planes(key, planes_ref, j)

    def last_score_step(js):
        kcs = [ki_ref[pl.ds(pl.multiple_of(j * t, t), t), :] for j in js]
        scores = [jnp.zeros((t, t), F32) for _ in js]
        for h in range(IDX_HEADS):
            w_h = wit[h:h + 1, :]
            scores = [sc + jnp.maximum(_dot_t(kc, qim_ref[h]), 0.0) * w_h for sc, kc in zip(scores, kcs)]
        for c, (j, score) in enumerate(zip(js, scores)):
            store_keys(j, score, c == len(js) - 1)

    n_kv = vbt_ref.shape[0]

    def raw_scores(n, buf):
        tiles = []
        for j in (2 * n, 2 * n + 1):
            kc = ki_ref[pl.ds(pl.multiple_of(jnp.minimum(j, n_kv - 1) * t, t), t), :]
            tiles += [_dot_t(kc, qim_ref[h]) for h in range(IDX_HEADS)]
        for k, tile in enumerate(tiles):
            buf[k] = tile

    def finish_scores(n, buf):
        for c, j in enumerate((2 * n, 2 * n + 1)):
            score = jnp.zeros((t, t), F32)
            for h in range(IDX_HEADS):
                score = score + jnp.maximum(buf[c * IDX_HEADS + h], 0.0) * wit[h:h + 1, :]
            store_keys(j, score, False)

    _ping_pong(i // 2, raw_scores, finish_scores, (sta_ref,), (stb_ref,))

    @pl.when(i % 2 == 1)
    def _():
        last_score_step([i - 1, i])

    @pl.when(i % 2 == 0)
    def _():
        last_score_step([i])
    for pad in range(KEY_PAD):
        keys_ref[n_ch + pad] = jnp.full((t, t), INT_MIN, I32)
        planes_ref[n_ch + pad, 0:KEY_BITS] = jnp.zeros((KEY_BITS, 8, t), I32)
        planes_ref[n_ch + pad, KEY_BITS] = jnp.full((8, t), -1, I32)
    n_pairs = (n_ch + 1) // 2

    def count(hits):
        def body(jj, acc):
            for j in (2 * jj, 2 * jj + 1):
                acc = acc + jnp.sum(hits(keys_ref[j], j).reshape(t // 8, 8, t), axis=0)
            return acc
        acc = lax.fori_loop(0, n_pairs, body, jnp.zeros((8, t), F32))
        return jnp.sum(acc, axis=0, keepdims=True)

    def bit_step(bi, carry):
        rem, n_cand, thr_bits, flip = carry
        prev = jnp.where(bi == 0, KEY_BITS, bi - 1)

        def body(jj, acc):
            for j in (2 * jj, 2 * jj + 1):
                cand = cand_ref[j] & (planes_ref[j, prev] ^ flip)
                cand_ref[j] = cand
                acc = acc + lax.population_count(cand & planes_ref[j, bi])
            return acc

        ones = lax.fori_loop(0, n_pairs, body, jnp.zeros((8, t), I32))
        n_ones = jnp.sum(ones.astype(F32), axis=0, keepdims=True)
        take = n_ones >= rem
        bit = jnp.int32(1) << (KEY_BITS - 1 - bi)
        return (jnp.where(take, rem, rem - n_ones), jnp.where(take, n_ones, n_cand - n_ones),
                jnp.where(take, thr_bits | bit, thr_bits), jnp.where(take, 0, -1))

    def reset_cand(j, c):
        cand_ref[j] = jnp.full((8, t), -1, I32)
        return c

    lax.fori_loop(0, 2 * n_pairs, reset_cand, 0)
    n_all = jnp.full((1, t), 2.0 * t, F32) * n_pairs.astype(F32)
    need, n_eq, thr_bits, _ = lax.fori_loop(
        0, KEY_BITS, bit_step, (jnp.full((1, t), float(top_k), F32), n_all,
                                jnp.zeros((1, t), I32), jnp.zeros((1, t), I32)))
    thr = thr_bits ^ jnp.int32(INT_MIN)

    tie = jnp.where(thr > INT_MIN, jnp.where(n_eq > need, 1.0, 0.0), 0.0)

    @pl.when(jnp.max(tie) > 0.0)
    def _():
        def key_pos(j):
            return j * t + lax.broadcasted_iota(I32, (t, t), 0)

        def idx_step(b, cut):
            cand = cut + (jnp.int32(1) << (idx_bits - 1 - b))
            cnt = count(lambda kc, j: jnp.where(kc == thr, jnp.where(key_pos(j) < cand, 1.0, 0.0), 0.0))
            return jnp.where(cnt < need, cand, cut)

        cut = lax.fori_loop(0, idx_bits, idx_step, jnp.zeros((1, t), I32))
        cut = jnp.where(tie > 0.0, cut, jnp.int32(2 ** 30))

        def demote(j, c):
            kc = keys_ref[j]
            demoted = jnp.where(key_pos(j) > cut, kc - 1, kc)
            keys_ref[j] = jnp.where(kc == thr, demoted, kc)
            return c

        lax.fori_loop(0, n_ch, demote, 0)

    thr_sel = jnp.maximum(thr, jnp.int32(INT_MIN + 1))

    _softmax_init(m_ref, acc_ref)
    _masked_heads(qb_ref, qbm_ref, HEAD_DIM)

    nq = vbt_ref.shape[0]

    def to_bias(j, c):
        bias = jnp.where(keys_ref[j] >= thr_sel, 0.0, NEG)
        keys_ref[j] = lax.bitcast_convert_type(bias, I32)
        return c

    lax.fori_loop(0, 2 * n_pairs, to_bias, 0)

    def chunks_of(n):
        return [(jnp.minimum(j, nq), jnp.minimum(j, nq - 1)) for j in (2 * n, 2 * n + 1)]

    def score_tiles(n):
        biases = [lax.bitcast_convert_type(keys_ref[jk], F32) for jk, _ in chunks_of(n)]
        ks = [kb_ref[pl.ds(pl.multiple_of(jv * t, t), t), :] for _, jv in chunks_of(n)]
        return [[_dot_t(k, qbm_ref[h]) + bias for bias, k in zip(biases, ks)] for h in range(N_HEADS)]

    def value_tiles(n):
        return [[vbt_ref[jv] for _, jv in chunks_of(n)]] * N_HEADS

    _pipelined_steps(n_pairs, score_tiles, value_tiles, list(range(N_HEADS)),
                     ((sta_ref, mxa_ref), (stb_ref, mxb_ref)), m_ref, acc_ref)
    for g in range(N_HEADS // 2):
        _store_pair(o_ref, g, _normalized(acc_ref, 2 * g), _normalized(acc_ref, 2 * g + 1))


def _dsa(qi, wit, qb, ki, kb, vbt, *, batch, seq):
    n = batch * seq
    t = TILE
    nq = seq // t
    top_k = min(IDX_TOPK_MAX, seq // 4)
    idx_bits = max(1, int(math.ceil(math.log2(seq))))
    qrow = lambda b, i: (b * nq + i, 0)
    kv = lambda b, i: (b, 0)
    return pl.pallas_call(
        functools.partial(_dsa_kernel, top_k=top_k, idx_bits=idx_bits),
        grid=(batch, nq),
        in_specs=[pl.BlockSpec((t, 256), qrow), pl.BlockSpec((BF16_ROWS, t), lambda b, i: (0, b * nq + i)),
                  pl.BlockSpec((t, 256), qrow),
                  pl.BlockSpec((seq, 256), kv), pl.BlockSpec((seq, LANES), kv),
                  pl.BlockSpec((nq, VT_ROWS, t), lambda b, i: (b, 0, 0))],
        out_specs=pl.BlockSpec((t, 256), qrow),
        out_shape=jax.ShapeDtypeStruct((n, 256), BF16),
        scratch_shapes=[pltpu.VMEM((nq + KEY_PAD, t, t), I32),
                        pltpu.VMEM((nq + KEY_PAD, KEY_BITS + 1, 8, t), I32),
                        pltpu.VMEM((nq + KEY_PAD, 8, t), I32),
                        pltpu.VMEM((IDX_HEADS, t, 2 * LANES), BF16),
                        pltpu.VMEM((N_HEADS, t, LANES), BF16),
                        pltpu.VMEM((N_HEADS, 1, t), F32), pltpu.VMEM((N_HEADS, VT_ROWS, t), F32),
                        pltpu.VMEM((2 * IDX_HEADS, t, t), F32), pltpu.VMEM((N_HEADS, 1, t), F32),
                        pltpu.VMEM((2 * IDX_HEADS, t, t), F32), pltpu.VMEM((N_HEADS, 1, t), F32)],
        compiler_params=_cparams(("arbitrary", "arbitrary")),
        name="dsa",
    )(qi, wit, qb, ki, kb, vbt)


def _merge_kernel(x_ref, g_ref, oa_ref, ob_ref, oc_ref, om_ref, wg_ref, bg_ref, wb_ref, wo_ref, y_ref):
    x = x_ref[...]
    d = x.shape[1]
    h = _rms(x, g_ref[...]).astype(BF16)
    acc = jnp.zeros(x.shape, F32)
    for nb, o_ref in enumerate((oa_ref, ob_ref, oc_ref, om_ref)):
        gate = jax.nn.sigmoid(_dot(h, wg_ref[:, nb * d:(nb + 1) * d]) + bg_ref[:, nb * d:(nb + 1) * d])
        acc = acc + gate * _dot(o_ref[...], wb_ref[nb])
    y_ref[...] = x + _dot(acc.astype(BF16), wo_ref[...])


def _merge(x2, g, oa, ob, oc, om, wg, bg, wb, wo):
    n, d = x2.shape
    tm = min(DENSE_ROWS, n)
    row = lambda i: (i, 0)
    fixed2 = lambda i: (0, 0)
    bw = oa.shape[1]
    once = pl.Buffered(1)
    return pl.pallas_call(
        _merge_kernel,
        grid=(n // tm,),
        in_specs=[pl.BlockSpec((tm, d), row), pl.BlockSpec((1, d), fixed2),
                  pl.BlockSpec((tm, bw), row), pl.BlockSpec((tm, bw), row),
                  pl.BlockSpec((tm, bw), row), pl.BlockSpec((tm, bw), row),
                  pl.BlockSpec((d, N_BRANCH * d), fixed2, pipeline_mode=once),
                  pl.BlockSpec((1, N_BRANCH * d), fixed2),
                  pl.BlockSpec((N_BRANCH, bw, d), lambda i: (0, 0, 0), pipeline_mode=once),
                  pl.BlockSpec((d, d), fixed2, pipeline_mode=once)],
        out_specs=pl.BlockSpec((tm, d), row),
        out_shape=jax.ShapeDtypeStruct((n, d), F32),
        compiler_params=_cparams(("arbitrary",)),
        name="merge",
    )(x2, g, oa, ob, oc, om, wg, bg, wb, wo)


def _ffn_kernel(x_ref, g_ref, wi_ref, wo_ref, gf_ref, y_ref, *, hidden, chunk, final):
    x = x_ref[...]
    h = _rms(x, g_ref[...]).astype(BF16)
    acc = jnp.zeros(x.shape, F32)
    for c in range(hidden // chunk):
        a = _dot(h, wi_ref[:, c * chunk:(c + 1) * chunk])
        b = _dot(h, wi_ref[:, hidden + c * chunk:hidden + (c + 1) * chunk])
        t = (a * jax.nn.sigmoid(a)) * b
        acc = acc + _dot(t.astype(BF16), wo_ref[c * chunk:(c + 1) * chunk, :])
    y = x + acc
    if final:
        y = _rms(y, gf_ref[...])
    y_ref[...] = y


def _ffn(x2, g, w_in, w_out, g_final, *, final):
    n, d = x2.shape
    tm = min(DENSE_ROWS, n)
    hidden = w_out.shape[0]
    chunk = hidden // 4 if (hidden // 4) % LANES == 0 else hidden
    row = lambda i: (i, 0)
    fixed2 = lambda i: (0, 0)
    once = pl.Buffered(1)
    return pl.pallas_call(
        functools.partial(_ffn_kernel, hidden=hidden, chunk=chunk, final=final),
        grid=(n // tm,),
        in_specs=[pl.BlockSpec((tm, d), row), pl.BlockSpec((1, d), fixed2),
                  pl.BlockSpec((d, 2 * hidden), fixed2, pipeline_mode=once),
                  pl.BlockSpec((hidden, d), fixed2, pipeline_mode=once),
                  pl.BlockSpec((1, d), fixed2)],
        out_specs=pl.BlockSpec((tm, d), row),
        out_shape=jax.ShapeDtypeStruct((n, d), F32),
        compiler_params=_cparams(("arbitrary",)),
        name="ffn",
    )(x2, g, w_in, w_out, g_final)


_IN_SPLITS = (256, 256, 256, 4, 256, 64, 64, 256, 32, 8, 256, 256, 256, 256)


def _pack_w_in(w):
    d = w.shape[0]
    offs = np.cumsum((0,) + _IN_SPLITS)
    qa, ka, va, fa, qb, kb, vb, qi, ki, wi, qc, kc, vc, qm = (
        w[:, offs[k]:offs[k + 1]] for k in range(len(_IN_SPLITS)))

    def pad_heads(a):
        a = a.reshape(d, N_HEADS, HEAD_DIM)
        return jnp.pad(a, ((0, 0), (0, 0), (0, LANES - HEAD_DIM))).reshape(d, N_HEADS * LANES)

    def pad_to(a, wd):
        return jnp.pad(a, ((0, 0), (0, wd - a.shape[1])))

    cols = [pad_heads(qa), pad_heads(ka), pad_to(jnp.tile(fa, (1, 3)), LANES), qb,
            jnp.concatenate([kb, kb], axis=1),
            qi, jnp.tile(ki, (1, IDX_HEADS)), qc, kc, qm]
    rows = [va, vc, vb, pad_to(wi, BF16_ROWS)]
    return (jnp.concatenate(cols, axis=1).astype(BF16), jnp.concatenate(rows, axis=1).T.astype(BF16))


def _fox_consts():
    e = np.zeros((LANES, 1024), np.float32)
    oq = np.zeros((1, 512), np.float32)
    ok = np.zeros((1, 512), np.float32)
    for h in range(N_HEADS):
        base = h * LANES + HEAD_DIM
        for t in range(3):
            e[N_HEADS * t + h, base + t] = 1.0
            e[N_HEADS * t + h, 512 + base + 3 + t] = -1.0
            oq[0, base + 3 + t] = 1.0
            ok[0, base + t] = 1.0
    return (jnp.asarray(e, BF16), jnp.asarray(oq), jnp.asarray(ok))


def _rope_tables(positions, rot_dim, period):
    half = rot_dim // 2
    inv_freq = ROPE_THETA ** (-jnp.arange(0, rot_dim, 2, dtype=F32) / rot_dim)
    ang = positions.astype(F32).reshape(-1, 1) * inv_freq
    cos, sin = jnp.cos(ang), jnp.sin(ang)
    n = ang.shape[0]
    c = jnp.concatenate([cos, cos, jnp.ones((n, period - 2 * half), F32)], axis=1)
    s = jnp.concatenate([-sin, sin, jnp.zeros((n, period - 2 * half), F32)], axis=1)
    reps = LANES // period
    return jnp.tile(c, (1, reps)), jnp.tile(s, (1, reps))


def kernel(x, mem, positions, norm_mix_g, w_in, b_forget, diff_lambda, diff_subln_g, norm_mem_g,
           w_mem_kv, w_branch, w_gate, b_gate, w_out, norm_ffn_g, w_ffn_in, w_ffn_out, final_norm_g):
    batch, seq, d = x.shape
    depth = w_in.shape[0]
    n = batch * seq
    assert seq % TILE == 0, seq

    c16, s16 = _rope_tables(positions, HEAD_DIM // 4, HEAD_DIM)
    c8, s8 = _rope_tables(positions, DIFF_QK_DIM // 4, DIFF_QK_DIM)
    consts = _fox_consts()

    x2 = x.reshape(n, d)
    for l in range(depth):
        lam_init = 0.8 - 0.6 * math.exp(-0.3 * l)
        bf = jnp.pad(jnp.tile(b_forget[l], 3), (0, LANES - 3 * b_forget.shape[1])).reshape(1, LANES)
        w_row, w_t = _pack_w_in(w_in[l])
        z = _proj(x2, norm_mix_g[l].reshape(1, d), w_row, w_t, bf, (c16, s16, c8, s8), consts, seq=seq)
        o_a = _fox(z["qa"], z["ka"], z["vaT"], batch=batch, seq=seq)
        o_b = _dsa(z["qi"], z["wiT"], z["qb"], z["ki"], z["kb"], z["vbT"], batch=batch, seq=seq)
        o_c = _diff(z["qc"], z["kc"], z["vcT"], diff_lambda[l], jnp.full((1, 1), lam_init, F32),
                    jnp.tile(diff_subln_g[l], 2).reshape(1, LANES), batch=batch, seq=seq)
        w_kv = w_mem_kv[l]
        o_m = _mem_attn(z["qm"], mem, norm_mem_g[l].reshape(1, d), w_kv[:, :256].astype(BF16),
                        w_kv[:, 256:].T.astype(BF16), batch=batch, seq=seq)
        x2 = _merge(x2, norm_mix_g[l].reshape(1, d), o_a, o_b, o_c, o_m,
                    w_gate[l].astype(BF16), b_gate[l].reshape(1, -1), w_branch[l].astype(BF16),
                    w_out[l].astype(BF16))
        x2 = _ffn(x2, norm_ffn_g[l].reshape(1, d), w_ffn_in[l].astype(BF16), w_ffn_out[l].astype(BF16),
                  final_norm_g.reshape(1, d), final=(l == depth - 1))
    return x2.reshape(batch, seq, d)
```
